```python
import math
import jax, jax.numpy as jnp
from jax import lax
import numpy as np

D_MODEL = 1024
BATCH = 8
SEQ = 2048
DEPTH = 4
DEC_BATCH = 32
DEC_SEQ = 1
PAST_LEN = 8192
PAGE_SIZE = 128

N_EVEN = (DEPTH + 1) // 2
N_ODD = DEPTH // 2
H_RET = 4
DK_RET = 128
DV_RET = 128
RET_CHUNK = 128
ROPE_BASE = 10000.0
S5_WIDTH = D_MODEL // 2
S5_GROUP = 16
S5_GROUPS = S5_WIDTH // S5_GROUP
S5_STATE = 64
H_DIFF = 4
DH_DIFF = 64
H_SB = 8
DH_SB = 64
Q_BLOCK = 128
N_BUCKETS = 32
MAX_DISTANCE = 128
D_FF = 2816
CONV_W = 3
EPS = 1e-6

RET_QK = H_RET * DK_RET
RET_V = H_RET * DV_RET
EVEN_IN = 2 * RET_QK + 2 * RET_V + S5_WIDTH
EVEN_MIX = RET_V + S5_WIDTH
DIFF_W = H_DIFF * 2 * DH_DIFF
SB_W = H_SB * DH_SB
ODD_IN = 3 * DIFF_W + 3 * SB_W
ODD_MIX = DIFF_W + SB_W

kernel_name = 'hybrid_retention_s5_diffattn_stickbreak_decoder_step'


def _rmsnorm(x, g):
    xf = x.astype(jnp.float32)
    y = xf * lax.rsqrt(jnp.mean(xf * xf, axis=-1, keepdims=True) + EPS)
    return (y * g.astype(jnp.float32)).astype(x.dtype)


def _modulation(c, w_ada, b_ada):
    mod = jax.nn.silu(c) @ w_ada + b_ada
    return [m[:, None, :] for m in jnp.split(mod, 6, axis=-1)]


def _rope(x, pos):
    half = x.shape[-1] // 2
    inv = jnp.power(ROPE_BASE, -jnp.arange(half, dtype=jnp.float32) / half)
    ang = pos.astype(jnp.float32)[:, None] * inv[None, :]
    cos = jnp.cos(ang)[None, :, None, :]
    sin = jnp.sin(ang)[None, :, None, :]
    xf = x.astype(jnp.float32)
    x1, x2 = xf[..., :half], xf[..., half:]
    return jnp.concatenate([x1 * cos - x2 * sin, x1 * sin + x2 * cos], axis=-1)


def _q_block(n):
    if n % Q_BLOCK == 0:
        return Q_BLOCK
    if n < Q_BLOCK:
        return n
    return math.gcd(n, Q_BLOCK)


def _t5_bucket(dist):
    n = jnp.maximum(dist, 0)
    max_exact = N_BUCKETS // 2
    large = max_exact + (jnp.log(jnp.maximum(n, 1).astype(jnp.float32) / max_exact)
                         / math.log(MAX_DISTANCE / max_exact) * (N_BUCKETS - max_exact)).astype(jnp.int32)
    large = jnp.minimum(large, N_BUCKETS - 1)
    return jnp.where(n < max_exact, n, large)


def _retention(q, k, v, s0):
    bsz, seq = q.shape[0], q.shape[1]
    c = math.gcd(seq, RET_CHUNK)
    nc = seq // c
    log_g = jnp.log1p(-jnp.exp2(-5.0 - jnp.arange(H_RET, dtype=jnp.float32)))
    idx = jnp.arange(c, dtype=jnp.float32)
    diff = idx[:, None] - idx[None, :]
    intra_decay = jnp.where(diff >= 0, jnp.exp(jnp.maximum(diff, 0.0)[None] * log_g[:, None, None]), 0.0)
    q_decay = jnp.exp((idx + 1.0)[None, :] * log_g[:, None])[..., None]
    k_decay = jnp.exp((c - 1.0 - idx)[None, :] * log_g[:, None])[..., None]
    chunk_decay = jnp.exp(c * log_g)[:, None, None]

    def to_chunks(t):
        return t.reshape(bsz, nc, c, H_RET, t.shape[-1]).transpose(1, 0, 3, 2, 4)

    def step(s, qkv):
        qc, kc, vc = qkv
        scores = jnp.einsum('bhid,bhjd->bhij', qc, kc) * intra_decay
        o = jnp.einsum('bhij,bhjv->bhiv', scores, vc) + jnp.einsum('bhid,bhdv->bhiv', qc, s) * q_decay
        s = s * chunk_decay + jnp.einsum('bhjd,bhjv->bhdv', kc * k_decay, vc)
        return s, o

    s, o = lax.scan(step, s0, (to_chunks(q), to_chunks(k), to_chunks(v)))
    o = o.transpose(1, 0, 3, 2, 4).reshape(bsz, seq, H_RET, DV_RET)
    return o, s


def _cplx_combine(e1, e2):
    a1r, a1i, b1r, b1i = e1
    a2r, a2i, b2r, b2i = e2
    return (a2r * a1r - a2i * a1i, a2r * a1i + a2i * a1r,
            a2r * b1r - a2i * b1i + b2r, a2r * b1i + a2i * b1r + b2i)


def _s5(u, lam_re, lam_im, log_step, b_re, b_im, c_re, c_im, d_skip, w_glu, x0):
    f32 = jnp.float32
    bsz, seq, _ = u.shape
    ug = u.reshape(bsz, seq, S5_GROUPS, S5_GROUP).astype(f32)
    lr, li = lam_re.astype(f32), lam_im.astype(f32)
    dt = jnp.exp(log_step.astype(f32))[:, None]
    mag = jnp.exp(lr * dt)
    a_re, a_im = mag * jnp.cos(li * dt), mag * jnp.sin(li * dt)
    den = lr * lr + li * li
    coef_re = ((a_re - 1.0) * lr + a_im * li) / den
    coef_im = (a_im * lr - (a_re - 1.0) * li) / den
    br, bi = b_re.astype(f32), b_im.astype(f32)
    bbar_re = coef_re[..., None] * br - coef_im[..., None] * bi
    bbar_im = coef_re[..., None] * bi + coef_im[..., None] * br
    bu_re = jnp.einsum('blgi,gpi->blgp', ug, bbar_re)
    bu_im = jnp.einsum('blgi,gpi->blgp', ug, bbar_im)
    if x0 is not None:
        x0r, x0i = x0[0].astype(f32), x0[1].astype(f32)
        bu_re = bu_re.at[:, 0].add(a_re * x0r - a_im * x0i)
        bu_im = bu_im.at[:, 0].add(a_re * x0i + a_im * x0r)
    ar = jnp.broadcast_to(a_re, (1, seq) + a_re.shape)
    ai = jnp.broadcast_to(a_im, (1, seq) + a_im.shape)
    _, _, xr, xi = lax.associative_scan(_cplx_combine, (ar, ai, bu_re, bu_im), axis=1)
    y = (jnp.einsum('blgp,gip->blgi', xr, c_re.astype(f32))
         - jnp.einsum('blgp,gip->blgi', xi, c_im.astype(f32))
         + d_skip.astype(f32)[None, None] * ug)
    y = jax.nn.gelu(y.reshape(bsz, seq, S5_WIDTH)).astype(u.dtype)
    y = y * jax.nn.sigmoid(y @ w_glu)
    return y, xr[:, -1], xi[:, -1]


def _even_mixer(h, pos0, w_in, w_out, gn_g, lam_re, lam_im, log_step, b_re, b_im, c_re, c_im, d_skip, w_glu, ret_s0, s5_x0):
    f32 = jnp.float32
    bsz, seq, _ = h.shape
    q, k, v, g, u = jnp.split(h @ w_in, [RET_QK, 2 * RET_QK, 2 * RET_QK + RET_V, 2 * RET_QK + 2 * RET_V], axis=-1)
    pos = pos0 + jnp.arange(seq)
    q = _rope(q.reshape(bsz, seq, H_RET, DK_RET), pos)
    k = _rope(k.reshape(bsz, seq, H_RET, DK_RET), pos) * (DK_RET ** -0.5)
    v = v.reshape(bsz, seq, H_RET, DV_RET).astype(f32)
    if ret_s0 is None:
        ret_s0 = jnp.zeros((bsz, H_RET, DK_RET, DV_RET), f32)
    o, ret_s = _retention(q, k, v, ret_s0.astype(f32))
    mu = jnp.mean(o, axis=-1, keepdims=True)
    var = jnp.mean(jnp.square(o - mu), axis=-1, keepdims=True)
    o = (o - mu) * lax.rsqrt(var + EPS) * gn_g.astype(f32)
    o = o.reshape(bsz, seq, RET_V).astype(h.dtype) * jax.nn.silu(g)
    y, xr, xi = _s5(u, lam_re, lam_im, log_step, b_re, b_im, c_re, c_im, d_skip, w_glu, s5_x0)
    out = jnp.concatenate([o, y], axis=-1) @ w_out
    return out, ret_s, xr, xi


def _diff_attention(q, k, v, rel_bias, lam, q_off):
    f32 = jnp.float32
    seq = q.shape[1]
    qb = _q_block(seq)
    scale = DH_DIFF ** -0.5
    q1, q2 = q[..., :DH_DIFF], q[..., DH_DIFF:]
    k1, k2 = k[..., :DH_DIFF], k[..., DH_DIFF:]
    outs = []
    for b0 in range(0, seq, qb):
        kend = q_off + b0 + qb
        qpos = q_off + b0 + jnp.arange(qb)
        rel = qpos[:, None] - jnp.arange(kend)[None, :]
        causal = rel >= 0
        bias = jnp.transpose(rel_bias[_t5_bucket(rel)], (2, 0, 1)).astype(f32)

        def probs(qh, kh):
            s = jnp.einsum('bqhd,bkhd->bhqk', qh[:, b0:b0 + qb], kh[:, :kend]).astype(f32) * scale + bias
            return jax.nn.softmax(jnp.where(causal, s, -jnp.inf), axis=-1)

        p = probs(q1, k1) - lam * probs(q2, k2)
        outs.append(jnp.einsum('bhqk,bkhd->bqhd', p.astype(v.dtype), v[:, :kend]))
    return jnp.concatenate(outs, axis=1)


def _stick_breaking(q, k, v, q_off):
    f32 = jnp.float32
    seq = q.shape[1]
    qb = _q_block(seq)
    scale = DH_SB ** -0.5
    outs = []
    for b0 in range(0, seq, qb):
        kend = q_off + b0 + qb
        qpos = q_off + b0 + jnp.arange(qb)
        strict = jnp.arange(kend)[None, :] < qpos[:, None]
        z = jnp.einsum('bqhd,bkhd->bhqk', q[:, b0:b0 + qb], k[:, :kend]).astype(f32) * scale
        log_keep = jnp.where(strict, jax.nn.log_sigmoid(-z), 0.0)
        after = lax.cumsum(log_keep, axis=3, reverse=True) - log_keep
        w = jnp.where(strict, jnp.exp(jax.nn.log_sigmoid(z) + after), 0.0)
        outs.append(jnp.einsum('bhqk,bkhd->bqhd', w.astype(v.dtype), v[:, :kend]))
    return jnp.concatenate(outs, axis=1)


def _odd_mixer(h, q_off, w_in, w_out, lam_vecs, subln_g, rel_bias, lam_init, past):
    f32 = jnp.float32
    bsz, seq, _ = h.shape
    dq, dk, dv, sq, sk, sv = jnp.split(h @ w_in, [DIFF_W, 2 * DIFF_W, 3 * DIFF_W, 3 * DIFF_W + SB_W, 3 * DIFF_W + 2 * SB_W], axis=-1)
    dq = dq.reshape(bsz, seq, H_DIFF, 2 * DH_DIFF)
    dk = dk.reshape(bsz, seq, H_DIFF, 2 * DH_DIFF)
    dv = dv.reshape(bsz, seq, H_DIFF, 2 * DH_DIFF)
    sq = sq.reshape(bsz, seq, H_SB, DH_SB)
    sk = sk.reshape(bsz, seq, H_SB, DH_SB)
    sv = sv.reshape(bsz, seq, H_SB, DH_SB)
    if past is None:
        kd, vd, ks, vs = dk, dv, sk, sv
    else:
        kd = jnp.concatenate([past[0].astype(dk.dtype), dk], axis=1)
        vd = jnp.concatenate([past[1].astype(dv.dtype), dv], axis=1)
        ks = jnp.concatenate([past[2].astype(sk.dtype), sk], axis=1)
        vs = jnp.concatenate([past[3].astype(sv.dtype), sv], axis=1)
    lv = lam_vecs.astype(f32)
    lam = jnp.exp(jnp.dot(lv[0], lv[1])) - jnp.exp(jnp.dot(lv[2], lv[3])) + lam_init
    od = _diff_attention(dq, kd, vd, rel_bias, lam, q_off)
    od = _rmsnorm(od, subln_g) * (1.0 - lam_init)
    osb = _stick_breaking(sq, ks, vs, q_off)
    out = jnp.concatenate([od.reshape(bsz, seq, DIFF_W), osb.reshape(bsz, seq, SB_W)], axis=-1) @ w_out
    return out, dk, dv, sk, sv


def _conv_ffn(h, buf, w_up, conv_w, conv_b, w_down):
    up = h @ w_up
    if buf is None:
        buf = jnp.zeros((h.shape[0], CONV_W - 1, up.shape[-1]), up.dtype)
    ext = jnp.concatenate([buf.astype(up.dtype), up], axis=1)
    mixed = lax.conv_general_dilated(ext, conv_w[:, None, :].astype(up.dtype), window_strides=(1,), padding='VALID',
                                     dimension_numbers=('NWC', 'WIO', 'NWC'),
                                     feature_group_count=up.shape[-1]) + conv_b
    val, gate = jnp.split(mixed, 2, axis=-1)
    return (val * jax.nn.gelu(gate)) @ w_down, ext[:, -(CONV_W - 1):]


def _gather_pages(cache, layer, page_table):
    g = cache[layer, page_table]
    return g.reshape((g.shape[0], g.shape[1] * g.shape[2]) + g.shape[3:])


def _run_trunk(x, c, q_off, past, P):
    new_ret, new_s5r, new_s5i, new_kd, new_vd, new_ks, new_vs, new_conv = ([] for _ in range(8))
    for li in range(DEPTH):
        sh1, sc1, g1, sh2, sc2, g2 = _modulation(c, P['w_ada'][li], P['b_ada'][li])
        h = _rmsnorm(x, P['g_pre_mix'][li]) * (1.0 + sc1) + sh1
        if li % 2 == 0:
            ie = li // 2
            s0 = None if past is None else past['ret'][ie]
            x0 = None if past is None else (past['s5_re'][ie], past['s5_im'][ie])
            m, rs, xr, xi = _even_mixer(h, q_off, P['w_in_even'][ie], P['w_out_even'][ie], P['ret_gn_g'][ie],
                                        P['s5_lam_re'][ie], P['s5_lam_im'][ie], P['s5_log_step'][ie],
                                        P['s5_b_re'][ie], P['s5_b_im'][ie], P['s5_c_re'][ie], P['s5_c_im'][ie],
                                        P['s5_d'][ie], P['s5_w_glu'][ie], s0, x0)
            new_ret.append(rs)
            new_s5r.append(xr)
            new_s5i.append(xi)
        else:
            io = li // 2
            kv = None
            if past is not None:
                pt = past['page_table']
                kv = (_gather_pages(past['diff_k'], io, pt), _gather_pages(past['diff_v'], io, pt),
                      _gather_pages(past['sb_k'], io, pt), _gather_pages(past['sb_v'], io, pt))
            lam_init = 0.8 - 0.6 * math.exp(-0.3 * li)
            m, kd, vd, ks, vs = _odd_mixer(h, q_off, P['w_in_odd'][io], P['w_out_odd'][io], P['diff_lam'][io],
                                           P['diff_subln_g'][io], P['rel_bias'], lam_init, kv)
            new_kd.append(kd)
            new_vd.append(vd)
            new_ks.append(ks)
            new_vs.append(vs)
        x = x + g1 * _rmsnorm(m, P['g_post_mix'][li])
        h = _rmsnorm(x, P['g_pre_ffn'][li]) * (1.0 + sc2) + sh2
        f, cb = _conv_ffn(h, None if past is None else past['conv'][li], P['w_up'][li], P['conv_w'][li],
                          P['conv_b'][li], P['w_down'][li])
        new_conv.append(cb)
        x = x + g2 * _rmsnorm(f, P['g_post_ffn'][li])
    st = lambda t: jnp.stack(t, axis=0)
    return x, (st(new_ret), st(new_s5r), st(new_s5i), st(new_kd), st(new_vd), st(new_ks), st(new_vs), st(new_conv))


def setup_inputs(seed: int = 0) -> dict:
    key = jax.random.key(seed)
    ks = iter(jax.random.split(key, 64))
    f32 = jnp.float32
    n_pages = PAST_LEN // PAGE_SIZE
    n_pool = (5 * DEC_BATCH * n_pages) // 4

    def nrm(shape, scale=1.0):
        return scale * jax.random.normal(next(ks), shape, f32)

    def gain(shape):
        return 1.0 + nrm(shape, 0.02)

    page_table = jax.random.permutation(next(ks), n_pool)[: DEC_BATCH * n_pages].reshape(DEC_BATCH, n_pages).astype(jnp.int32)
    return {
        'x_prompt': nrm((BATCH, SEQ, D_MODEL)),
        'x_sample': nrm((DEC_BATCH, DEC_SEQ, D_MODEL)),
        'c_prompt': nrm((BATCH, D_MODEL)),
        'c_sample': nrm((DEC_BATCH, D_MODEL)),
        'state_ret': nrm((N_EVEN, DEC_BATCH, H_RET, DK_RET, DV_RET), 0.1),
        'state_s5_re': nrm((N_EVEN, DEC_BATCH, S5_GROUPS, S5_STATE), 0.1),
        'state_s5_im': nrm((N_EVEN, DEC_BATCH, S5_GROUPS, S5_STATE), 0.1),
        'cache_diff_k': nrm((N_ODD, n_pool, PAGE_SIZE, H_DIFF, 2 * DH_DIFF)),
        'cache_diff_v': nrm((N_ODD, n_pool, PAGE_SIZE, H_DIFF, 2 * DH_DIFF)),
        'cache_sb_k': nrm((N_ODD, n_pool, PAGE_SIZE, H_SB, DH_SB)),
        'cache_sb_v': nrm((N_ODD, n_pool, PAGE_SIZE, H_SB, DH_SB)),
        'state_conv': nrm((DEPTH, DEC_BATCH, CONV_W - 1, 2 * D_FF)),
        'page_table': page_table,
        'w_ada': nrm((DEPTH, D_MODEL, 6 * D_MODEL), 0.5 * D_MODEL ** -0.5),
        'b_ada': nrm((DEPTH, 6 * D_MODEL), 0.01),
        'g_pre_mix': gain((DEPTH, D_MODEL)),
        'g_post_mix': gain((DEPTH, D_MODEL)),
        'g_pre_ffn': gain((DEPTH, D_MODEL)),
        'g_post_ffn': gain((DEPTH, D_MODEL)),
        'w_in_even': nrm((N_EVEN, D_MODEL, EVEN_IN), D_MODEL ** -0.5),
        'w_out_even': nrm((N_EVEN, EVEN_MIX, D_MODEL), EVEN_MIX ** -0.5),
        'ret_gn_g': gain((N_EVEN, H_RET, DV_RET)),
        's5_lam_re': -0.5 + nrm((N_EVEN, S5_GROUPS, S5_STATE), 0.01),
        's5_lam_im': jnp.pi * jnp.arange(S5_STATE, dtype=f32)[None, None, :] + nrm((N_EVEN, S5_GROUPS, S5_STATE), 0.01),
        's5_log_step': jax.random.uniform(next(ks), (N_EVEN, S5_GROUPS), f32, math.log(1e-3), math.log(1e-1)),
        's5_b_re': nrm((N_EVEN, S5_GROUPS, S5_STATE, S5_GROUP), (2 * S5_GROUP) ** -0.5),
        's5_b_im': nrm((N_EVEN, S5_GROUPS, S5_STATE, S5_GROUP), (2 * S5_GROUP) ** -0.5),
        's5_c_re': nrm((N_EVEN, S5_GROUPS, S5_GROUP, S5_STATE), S5_STATE ** -0.5),
        's5_c_im': nrm((N_EVEN, S5_GROUPS, S5_GROUP, S5_STATE), S5_STATE ** -0.5),
        's5_d': nrm((N_EVEN, S5_GROUPS, S5_GROUP)),
        's5_w_glu': nrm((N_EVEN, S5_WIDTH, S5_WIDTH), S5_WIDTH ** -0.5),
        'w_in_odd': nrm((N_ODD, D_MODEL, ODD_IN), D_MODEL ** -0.5),
        'w_out_odd': nrm((N_ODD, ODD_MIX, D_MODEL), ODD_MIX ** -0.5),
        'diff_lam': nrm((N_ODD, 4, DH_DIFF), 0.1),
        'diff_subln_g': gain((N_ODD, 2 * DH_DIFF)),
        'rel_bias': nrm((N_BUCKETS, H_DIFF), 0.5),
        'w_up': nrm((DEPTH, D_MODEL, 2 * D_FF), D_MODEL ** -0.5),
        'conv_w': nrm((DEPTH, CONV_W, 2 * D_FF), CONV_W ** -0.5),
        'conv_b': nrm((DEPTH, 2 * D_FF), 0.01),
        'w_down': nrm((DEPTH, D_FF, D_MODEL), D_FF ** -0.5),
    }


def reference(x_prompt, x_sample, c_prompt, c_sample, state_ret, state_s5_re, state_s5_im,
              cache_diff_k, cache_diff_v, cache_sb_k, cache_sb_v, state_conv, page_table,
              w_ada, b_ada, g_pre_mix, g_post_mix, g_pre_ffn, g_post_ffn,
              w_in_even, w_out_even, ret_gn_g, s5_lam_re, s5_lam_im, s5_log_step,
              s5_b_re, s5_b_im, s5_c_re, s5_c_im, s5_d, s5_w_glu,
              w_in_odd, w_out_odd, diff_lam, diff_subln_g, rel_bias,
              w_up, conv_w, conv_b, w_down):
    P = dict(w_ada=w_ada, b_ada=b_ada, g_pre_mix=g_pre_mix, g_post_mix=g_post_mix, g_pre_ffn=g_pre_ffn,
             g_post_ffn=g_post_ffn, w_in_even=w_in_even, w_out_even=w_out_even, ret_gn_g=ret_gn_g,
             s5_lam_re=s5_lam_re, s5_lam_im=s5_lam_im, s5_log_step=s5_log_step, s5_b_re=s5_b_re,
             s5_b_im=s5_b_im, s5_c_re=s5_c_re, s5_c_im=s5_c_im, s5_d=s5_d, s5_w_glu=s5_w_glu,
             w_in_odd=w_in_odd, w_out_odd=w_out_odd, diff_lam=diff_lam, diff_subln_g=diff_subln_g,
             rel_bias=rel_bias, w_up=w_up, conv_w=conv_w, conv_b=conv_b, w_down=w_down)
    past = dict(ret=state_ret, s5_re=state_s5_re, s5_im=state_s5_im, diff_k=cache_diff_k, diff_v=cache_diff_v,
                sb_k=cache_sb_k, sb_v=cache_sb_v, conv=state_conv, page_table=page_table)
    y_prompt, sp = _run_trunk(x_prompt, c_prompt, 0, None, P)
    y_sample, ss = _run_trunk(x_sample, c_sample, PAST_LEN, past, P)
    ret_p, s5r_p, s5i_p, kd_p, vd_p, ks_p, vs_p, conv_p = sp
    ret_s, s5r_s, s5i_s, kd_s, vd_s, ks_s, vs_s, conv_s = ss
    return (y_prompt, y_sample, ret_p, ret_s, s5r_p, s5r_s, s5i_p, s5i_s, kd_p, kd_s, vd_p, vd_s,
            ks_p, ks_s, vs_p, vs_s, conv_p, conv_s)
```

```python
import functools
import math

import jax
import jax.numpy as jnp
from jax import lax
from jax.experimental import pallas as pl
from jax.experimental.pallas import tpu as pltpu

f32 = jnp.float32
bf16 = jnp.bfloat16

D_MODEL = 1024
DEPTH = 4
PAST_LEN = 8192
PAGE_SIZE = 128
H_RET = 4
DK_RET = 128
DV_RET = 128
RET_CHUNK = 128
ROPE_BASE = 10000.0
S5_WIDTH = D_MODEL // 2
S5_GROUP = 16
S5_GROUPS = S5_WIDTH // S5_GROUP
S5_STATE = 64
S5_N = S5_GROUPS * S5_STATE
H_DIFF = 4
DH_DIFF = 64
H_SB = 8
DH_SB = 64
N_BUCKETS = 32
MAX_DISTANCE = 128
D_FF = 2816
CONV_W = 3
EPS = 1e-6
MIX_W = 512

LANES = 128
SUBLANES = 8
VMEM_LIMIT = 56 * 1024 * 1024
NEG_BIG = -1e30


def _cparams(n_axes):
    return pltpu.CompilerParams(dimension_semantics=("arbitrary",) * n_axes, vmem_limit_bytes=VMEM_LIMIT)


def _split(a):
    hi = a.astype(bf16)
    lo = (a - hi.astype(f32)).astype(bf16)
    return hi, lo


def _lhs(a, passes):
    if passes == 1:
        return (a.astype(bf16),)
    return _split(a.astype(f32))


def _dot(lhs, b, passes):
    if passes == 1:
        return jnp.dot(lhs[0], b.astype(bf16), preferred_element_type=f32)
    b_hi, b_lo = _split(b.astype(f32))
    return jnp.dot(lhs[0], b_hi, preferred_element_type=f32) + (
        jnp.dot(lhs[0], b_lo, preferred_element_type=f32) + jnp.dot(lhs[1], b_hi, preferred_element_type=f32)
    )


def _dot_nt(a, b):
    return lax.dot_general(a, b, (((1,), (1,)), ((), ())), preferred_element_type=f32)


def _norm_mod(x, g, sc, sh):
    ms = jnp.mean(x * x, axis=-1, keepdims=True)
    return (x * lax.rsqrt(ms + EPS) * g) * (1.0 + sc) + sh


def _gelu(x):
    return 0.5 * x * (1.0 + jnp.tanh(math.sqrt(2.0 / math.pi) * (x + 0.044715 * (x * x * x))))


def _sigmoid(x):
    return 1.0 / (1.0 + jnp.exp(-x))


def _silu(x):
    return x * _sigmoid(x)


def _ada_kernel(c_ref, w_ref, b_ref, o_ref):
    s = _silu(c_ref[...])
    o_ref[...] = _dot(_lhs(s, 3), w_ref[...], 3) + b_ref[...]


def _ada(c_all, w_ada, b_ada):
    n = c_all.shape[0]
    tn = 1024
    return pl.pallas_call(
        _ada_kernel,
        out_shape=jax.ShapeDtypeStruct((DEPTH, n, 6 * D_MODEL), f32),
        grid=(DEPTH, 6 * D_MODEL // tn),
        in_specs=[
            pl.BlockSpec((n, D_MODEL), lambda l, j: (0, 0)),
            pl.BlockSpec((None, D_MODEL, tn), lambda l, j: (l, 0, j)),
            pl.BlockSpec((None, 1, tn), lambda l, j: (l, 0, j)),
        ],
        out_specs=pl.BlockSpec((None, n, tn), lambda l, j: (l, 0, j)),
        compiler_params=_cparams(2),
        name="ada_mod",
    )(c_all, w_ada, b_ada.reshape(DEPTH, 1, 6 * D_MODEL))


class _Trunk:
    def __init__(self, mod, seq, tm, passes):
        self.mod = mod
        self.seq = seq
        self.tm = tm
        self.passes = passes
        self.per_row = seq == 1

    def mod_spec(self, li, k):
        if self.per_row:
            return pl.BlockSpec((None, self.tm, D_MODEL), lambda i: (li, i, k))
        tiles_per_seq = self.seq // self.tm
        return pl.BlockSpec((None, None, 1, D_MODEL), lambda i: (li, i // tiles_per_seq, 0, k))

    def row_spec(self, width):
        return pl.BlockSpec((self.tm, width), lambda i: (i, 0))


def _const_spec(shape, idx):
    return pl.BlockSpec(shape, lambda *_: idx)


def _rope(a, cos2, sin2):
    return a * cos2 + pltpu.roll(a, DK_RET // 2, 1) * sin2


def _proj_even_kernel(x_ref, sc_ref, sh_ref, g_ref, w_ref, cos_ref, sin_ref, q_ref, k_ref, v_ref, gg_ref, u_ref, *, passes):
    h = _lhs(_norm_mod(x_ref[...], g_ref[...], sc_ref[...], sh_ref[...]), passes)
    cos2 = cos_ref[...]
    sin2 = sin_ref[...]
    outs = (q_ref, k_ref, v_ref, gg_ref, u_ref)
    for n, ref in enumerate(outs):
        acc = _dot(h, w_ref[:, n * MIX_W:(n + 1) * MIX_W], passes)
        if n < 2:
            scale = 1.0 if n == 0 else DK_RET ** -0.5
            for hh in range(H_RET):
                sl = slice(hh * DK_RET, (hh + 1) * DK_RET)
                ref[:, sl] = (_rope(acc[:, sl], cos2, sin2) * scale).astype(ref.dtype)
        else:
            ref[...] = acc.astype(ref.dtype)


def _proj_even(tr, x, li, g_pre, w_in, cos2, sin2, out_dtypes):
    t = x.shape[0]
    tm = tr.tm
    n_in = w_in.shape[-1]
    if tr.per_row:
        cs_spec = pl.BlockSpec((1, DK_RET), lambda i: (0, 0))
    else:
        tiles = tr.seq // tm
        cs_spec = pl.BlockSpec((tm, DK_RET), lambda i: (i % tiles, 0))
    return pl.pallas_call(
        functools.partial(_proj_even_kernel, passes=tr.passes),
        out_shape=[jax.ShapeDtypeStruct((t, MIX_W), dt) for dt in out_dtypes],
        grid=(t // tm,),
        in_specs=[
            tr.row_spec(D_MODEL),
            tr.mod_spec(li, 1),
            tr.mod_spec(li, 0),
            _const_spec((None, 1, D_MODEL), (li, 0, 0)),
            _const_spec((None, D_MODEL, n_in), (li // 2, 0, 0)),
            cs_spec,
            cs_spec,
        ],
        out_specs=[tr.row_spec(MIX_W) for _ in out_dtypes],
        compiler_params=_cparams(1),
        name="proj_even",
    )(x, tr.mod, tr.mod, g_pre, w_in, cos2, sin2)


def _proj_odd_kernel(x_ref, sc_ref, sh_ref, g_ref, w_ref, *out_refs, passes, with_copies):
    h = _lhs(_norm_mod(x_ref[...], g_ref[...], sc_ref[...], sh_ref[...]), passes)
    for n in range(6):
        acc = _dot(h, w_ref[:, n * MIX_W:(n + 1) * MIX_W], passes)
        if n in (0, 3):
            acc = acc * (DH_DIFF ** -0.5 if n == 0 else DH_SB ** -0.5)
        out_refs[n][...] = acc.astype(out_refs[n].dtype)
        if with_copies and n not in (0, 3):
            out_refs[6 + (n if n < 3 else n - 1) - 1][...] = acc.astype(bf16)


def _proj_odd(tr, x, li, g_pre, w_in, with_copies):
    t = x.shape[0]
    q_dt = bf16 if with_copies else f32
    dts = [q_dt, f32, f32, q_dt, f32, f32] + ([bf16] * 4 if with_copies else [])
    return pl.pallas_call(
        functools.partial(_proj_odd_kernel, passes=tr.passes, with_copies=with_copies),
        out_shape=[jax.ShapeDtypeStruct((t, MIX_W), dt) for dt in dts],
        grid=(t // tr.tm,),
        in_specs=[
            tr.row_spec(D_MODEL),
            tr.mod_spec(li, 1),
            tr.mod_spec(li, 0),
            _const_spec((None, 1, D_MODEL), (li, 0, 0)),
            _const_spec((None, D_MODEL, 6 * MIX_W), (li // 2, 0, 0)),
        ],
        out_specs=[tr.row_spec(MIX_W) for _ in dts],
        compiler_params=_cparams(1),
        name="proj_odd",
    )(x, tr.mod, tr.mod, g_pre, w_in)


def _outproj_kernel(*refs, n_in, passes):
    a_refs = refs[:n_in]
    w_refs = refs[n_in:2 * n_in]
    x_ref, gate_ref, gp_ref, o_ref = refs[2 * n_in:]
    acc = _dot(_lhs(a_refs[0][...], passes), w_refs[0][...], passes)
    for a_ref, w_ref in zip(a_refs[1:], w_refs[1:]):
        acc = acc + _dot(_lhs(a_ref[...], passes), w_ref[...], passes)
    ms = jnp.mean(acc * acc, axis=-1, keepdims=True)
    y = acc * lax.rsqrt(ms + EPS) * gp_ref[...]
    o_ref[...] = x_ref[...] + gate_ref[...] * y


def _outproj(tr, acts, w, w_idx, x, li, gate_k, g_post):
    t = x.shape[0]
    widths = [a.shape[1] for a in acts]
    assert all(wd == widths[0] for wd in widths)
    w_specs = [pl.BlockSpec((None, widths[0], D_MODEL), lambda i, n=n: (w_idx, n, 0)) for n in range(len(acts))]
    return pl.pallas_call(
        functools.partial(_outproj_kernel, n_in=len(acts), passes=tr.passes),
        out_shape=jax.ShapeDtypeStruct((t, D_MODEL), f32),
        grid=(t // tr.tm,),
        in_specs=[tr.row_spec(wd) for wd in widths] + w_specs + [
            tr.row_spec(D_MODEL),
            tr.mod_spec(li, gate_k),
            _const_spec((None, 1, D_MODEL), (li, 0, 0)),
        ],
        out_specs=tr.row_spec(D_MODEL),
        compiler_params=_cparams(1),
        name="outproj",
    )(*acts, *([w] * len(acts)), x, tr.mod, g_post)


def _retention_kernel(q_ref, k_ref, v_ref, g_ref, intra_ref, qd_ref, kd_ref, gn_ref, o_ref, s_out_ref, s_scr, *, chunk_decay):
    c = pl.program_id(1)

    @pl.when(c == 0)
    def _():
        s_scr[...] = jnp.zeros_like(s_scr)

    for hh in range(H_RET):
        sl = slice(hh * DK_RET, (hh + 1) * DK_RET)
        qh = q_ref[:, sl]
        kh = k_ref[:, sl]
        vh = v_ref[:, sl]
        s_old = s_scr[hh]
        scores = _dot_nt(qh, kh) * intra_ref[hh]
        o = jnp.dot(scores.astype(bf16), vh, preferred_element_type=f32)
        o = o + jnp.dot(qh, s_old.astype(bf16), preferred_element_type=f32) * qd_ref[hh]
        kt = (kh.astype(f32) * kd_ref[hh]).T.astype(bf16)
        s_scr[hh] = s_old * chunk_decay[hh] + jnp.dot(kt, vh, preferred_element_type=f32)
        mu = jnp.mean(o, axis=-1, keepdims=True)
        d = o - mu
        var = jnp.mean(d * d, axis=-1, keepdims=True)
        on = d * lax.rsqrt(var + EPS) * gn_ref[hh]
        o_ref[:, sl] = (on * _silu(g_ref[:, sl])).astype(o_ref.dtype)

    @pl.when(c == pl.num_programs(1) - 1)
    def _():
        s_out_ref[...] = s_scr[...]


def _retention_tables(c):
    log_g = jnp.log1p(-jnp.exp2(-5.0 - jnp.arange(H_RET, dtype=f32)))
    idx = jnp.arange(c, dtype=f32)
    diff = idx[:, None] - idx[None, :]
    intra = jnp.where(diff >= 0, jnp.exp(jnp.maximum(diff, 0.0)[None] * log_g[:, None, None]), 0.0)
    q_decay = jnp.exp((idx + 1.0)[None, :] * log_g[:, None])[..., None]
    k_decay = jnp.exp((c - 1.0 - idx)[None, :] * log_g[:, None])[..., None]
    qd = jnp.broadcast_to(q_decay, (H_RET, c, DV_RET))
    kd = jnp.broadcast_to(k_decay, (H_RET, c, DK_RET))
    return intra, qd, kd


def _chunk_decay(c):
    return tuple(float(math.exp(c * math.log1p(-(2.0 ** (-5.0 - h))))) for h in range(H_RET))


def _retention(q, k, v, g, gn_g, bsz, seq):
    c = RET_CHUNK
    nc = seq // c
    intra, qd, kd = _retention_tables(c)
    row = pl.BlockSpec((c, MIX_W), lambda b, j: (b * nc + j, 0))
    tab = _const_spec((H_RET, c, c), (0, 0, 0))
    return pl.pallas_call(
        functools.partial(_retention_kernel, chunk_decay=_chunk_decay(c)),
        out_shape=[
            jax.ShapeDtypeStruct((bsz * seq, MIX_W), bf16),
            jax.ShapeDtypeStruct((bsz, H_RET, DK_RET, DV_RET), f32),
        ],
        grid=(bsz, nc),
        in_specs=[row, row, row, row, tab, tab, tab, _const_spec((H_RET, 1, DV_RET), (0, 0, 0))],
        out_specs=[row, pl.BlockSpec((None, H_RET, DK_RET, DV_RET), lambda b, j: (b, 0, 0, 0))],
        scratch_shapes=[pltpu.VMEM((H_RET, DK_RET, DV_RET), f32)],
        compiler_params=_cparams(2),
        name="retention",
    )(q, k, v, g, intra, qd, kd, gn_g.reshape(H_RET, 1, DV_RET))


SCAN_LW = 512
U_GROUPS = S5_WIDTH // LANES
ST_PER_UG = S5_N // U_GROUPS


def _s5_params(lam_re, lam_im, log_step, b_re, b_im, c_re, c_im):
    dt = jnp.exp(log_step)[:, None]
    mag = jnp.exp(lam_re * dt)
    a_re, a_im = mag * jnp.cos(lam_im * dt), mag * jnp.sin(lam_im * dt)
    den = lam_re * lam_re + lam_im * lam_im
    coef_re = ((a_re - 1.0) * lam_re + a_im * lam_im) / den
    coef_im = (a_im * lam_re - (a_re - 1.0) * lam_im) / den
    bbar_re = coef_re[..., None] * b_re - coef_im[..., None] * b_im
    bbar_im = coef_re[..., None] * b_im + coef_im[..., None] * b_re
    gpb = LANES // S5_GROUP
    eye = jnp.eye(gpb, dtype=f32)

    def b_layout(bb):
        bb = bb.reshape(U_GROUPS, gpb, S5_STATE, S5_GROUP).transpose(0, 1, 3, 2)
        return jnp.einsum("jgip,gh->jgihp", bb, eye).reshape(U_GROUPS, LANES, ST_PER_UG)

    def c_layout(cc):
        cc = cc.reshape(U_GROUPS, gpb, S5_GROUP, S5_STATE)
        return jnp.einsum("jgip,gh->jgphi", cc, eye).reshape(U_GROUPS, ST_PER_UG, LANES)

    bw = jnp.concatenate([b_layout(bbar_re), b_layout(bbar_im)], axis=-1)
    cw = jnp.concatenate([c_layout(c_re), -c_layout(c_im)], axis=1)
    return a_re.reshape(1, S5_N), a_im.reshape(1, S5_N), bw, cw


def _cmul(ar, ai, br, bi):
    return ar * br - ai * bi, ar * bi + ai * br


def _s5_scan_tables(a_re, a_im):
    rows = jnp.arange(SUBLANES)[:, None]
    tabs = []
    pr, pi = a_re, a_im
    pows = {}
    cr, ci = a_re, a_im
    carry_r, carry_i = [cr], [ci]
    for _ in range(SUBLANES - 1):
        cr, ci = _cmul(cr, ci, a_re, a_im)
        carry_r.append(cr)
        carry_i.append(ci)
    for d in (1, 2, 4):
        pows[d] = (carry_r[d - 1], carry_i[d - 1])
    for d in (1, 2, 4):
        tabs.append(jnp.where(rows >= d, pows[d][0], 0.0))
        tabs.append(jnp.where(rows >= d, pows[d][1], 0.0))
    tabs.append(jnp.concatenate(carry_r, axis=0))
    tabs.append(jnp.concatenate(carry_i, axis=0))
    del pr, pi
    return jnp.stack(tabs, axis=0)


def _s5_kernel(u_ref, bw_ref, tab_ref, cw_ref, d_ref, glu_ref, y_ref, xr_out, xi_out, xr_scr, xi_scr, car_scr, *, ts):
    t = pl.program_id(1)

    @pl.when(t == 0)
    def _():
        car_scr[...] = jnp.zeros_like(car_scr)

    u = u_ref[...]
    ub = u.astype(bf16)
    for j in range(U_GROUPS):
        bu = jnp.dot(ub[:, j * LANES:(j + 1) * LANES], bw_ref[j], preferred_element_type=f32)
        xr_scr[:, j * ST_PER_UG:(j + 1) * ST_PER_UG] = bu[:, :ST_PER_UG]
        xi_scr[:, j * ST_PER_UG:(j + 1) * ST_PER_UG] = bu[:, ST_PER_UG:]

    def group(r, carry):
        rows = pl.ds(pl.multiple_of(r * SUBLANES, SUBLANES), SUBLANES)
        for lc in range(S5_N // SCAN_LW):
            ls = slice(lc * SCAN_LW, (lc + 1) * SCAN_LW)
            xr = xr_scr[rows, ls]
            xi = xi_scr[rows, ls]
            for n, d in enumerate((1, 2, 4)):
                sr = pltpu.roll(xr, d, 0)
                si = pltpu.roll(xi, d, 0)
                are = tab_ref[2 * n, :, ls]
                aim = tab_ref[2 * n + 1, :, ls]
                xr, xi = xr + (are * sr - aim * si), xi + (are * si + aim * sr)
            cr = car_scr[0:1, ls]
            ci = car_scr[1:2, ls]
            are = tab_ref[6, :, ls]
            aim = tab_ref[7, :, ls]
            xr, xi = xr + (are * cr - aim * ci), xi + (are * ci + aim * cr)
            xr_scr[rows, ls] = xr
            xi_scr[rows, ls] = xi
            car_scr[0:1, ls] = xr[SUBLANES - 1:SUBLANES]
            car_scr[1:2, ls] = xi[SUBLANES - 1:SUBLANES]
        return carry

    lax.fori_loop(0, ts // SUBLANES, group, 0)

    ys = []
    for j in range(U_GROUPS):
        ss = slice(j * ST_PER_UG, (j + 1) * ST_PER_UG)
        yj = jnp.dot(xr_scr[:, ss].astype(bf16), cw_ref[j, :ST_PER_UG, :], preferred_element_type=f32)
        yj = yj + jnp.dot(xi_scr[:, ss].astype(bf16), cw_ref[j, ST_PER_UG:, :], preferred_element_type=f32)
        ys.append(yj)
    y = _gelu(jnp.concatenate(ys, axis=1) + d_ref[...] * u)
    z = jnp.dot(y.astype(bf16), glu_ref[...], preferred_element_type=f32)
    y_ref[...] = (y * _sigmoid(z)).astype(y_ref.dtype)

    @pl.when(t == pl.num_programs(1) - 1)
    def _():
        xr_out[...] = car_scr[0:1, :]
        xi_out[...] = car_scr[1:2, :]


def _s5(u, bw, tabs, cw, d_skip, w_glu, bsz, seq, ts):
    nt = seq // ts
    row = pl.BlockSpec((ts, S5_WIDTH), lambda b, j: (b * nt + j, 0))
    st = pl.BlockSpec((None, 1, S5_N), lambda b, j: (b, 0, 0))
    y, xr, xi = pl.pallas_call(
        functools.partial(_s5_kernel, ts=ts),
        out_shape=[
            jax.ShapeDtypeStruct((bsz * seq, S5_WIDTH), bf16),
            jax.ShapeDtypeStruct((bsz, 1, S5_N), f32),
            jax.ShapeDtypeStruct((bsz, 1, S5_N), f32),
        ],
        grid=(bsz, nt),
        in_specs=[
            row,
            _const_spec((U_GROUPS, LANES, 2 * ST_PER_UG), (0, 0, 0)),
            _const_spec((8, SUBLANES, S5_N), (0, 0, 0)),
            _const_spec((U_GROUPS, 2 * ST_PER_UG, LANES), (0, 0, 0)),
            _const_spec((1, S5_WIDTH), (0, 0)),
            _const_spec((S5_WIDTH, S5_WIDTH), (0, 0)),
        ],
        out_specs=[row, st, st],
        scratch_shapes=[pltpu.VMEM((ts, S5_N), f32), pltpu.VMEM((ts, S5_N), f32), pltpu.VMEM((SUBLANES, S5_N), f32)],
        compiler_params=_cparams(2),
        name="s5",
    )(u, bw, tabs, cw, d_skip, w_glu)
    return y, xr.reshape(bsz, S5_GROUPS, S5_STATE), xi.reshape(bsz, S5_GROUPS, S5_STATE)


FF_CHUNK = 256


def _ffn_up_kernel(x_ref, xh_ref, sc_ref, sh_ref, g_ref, w_ref, cw_ref, cb_ref, act_ref, cs_ref, *, tiles_per_seq):
    i = pl.program_id(0)
    tm = x_ref.shape[0]
    g = g_ref[...]
    sc = sc_ref[...]
    sh = sh_ref[...]
    h = _norm_mod(x_ref[...], g, sc, sh).astype(bf16)
    hh = _norm_mod(xh_ref[...], g, sc, sh).astype(bf16)
    keep = (i % tiles_per_seq != 0).astype(f32)
    for n in range(D_FF // FF_CHUNK):
        mixed = []
        for half in range(2):
            cs = slice(half * D_FF + n * FF_CHUNK, half * D_FF + (n + 1) * FF_CHUNK)
            w = w_ref[:, cs]
            up = jnp.dot(h, w, preferred_element_type=f32)
            halo = jnp.dot(hh, w, preferred_element_type=f32) * keep
            ext = jnp.concatenate([halo, up], axis=0)
            r1 = pltpu.roll(ext, 1, 0)[SUBLANES:]
            r2 = pltpu.roll(ext, 2, 0)[SUBLANES:]
            mixed.append(cw_ref[0:1, cs] * r2 + cw_ref[1:2, cs] * r1 + cw_ref[2:3, cs] * up + cb_ref[:, cs])
            cs_ref[:, cs] = up[tm - (CONV_W - 1):, :]
        act_ref[:, n * FF_CHUNK:(n + 1) * FF_CHUNK] = (mixed[0] * _gelu(mixed[1])).astype(act_ref.dtype)


def _ffn_up(tr, x, li, g_pre, w_up, conv_w, conv_b, bsz):
    t = x.shape[0]
    tm = tr.tm
    tiles = tr.seq // tm
    halo_blocks = tm // SUBLANES
    return pl.pallas_call(
        functools.partial(_ffn_up_kernel, tiles_per_seq=tiles),
        out_shape=[
            jax.ShapeDtypeStruct((t, D_FF), bf16),
            jax.ShapeDtypeStruct((bsz, CONV_W - 1, 2 * D_FF), f32),
        ],
        grid=(t // tm,),
        in_specs=[
            tr.row_spec(D_MODEL),
            pl.BlockSpec((SUBLANES, D_MODEL), lambda i: (jnp.maximum(i * halo_blocks - 1, 0), 0)),
            tr.mod_spec(li, 4),
            tr.mod_spec(li, 3),
            _const_spec((None, 1, D_MODEL), (li, 0, 0)),
            _const_spec((None, D_MODEL, 2 * D_FF), (li, 0, 0)),
            _const_spec((None, CONV_W, 2 * D_FF), (li, 0, 0)),
            _const_spec((None, 1, 2 * D_FF), (li, 0, 0)),
        ],
        out_specs=[
            tr.row_spec(D_FF),
            pl.BlockSpec((None, CONV_W - 1, 2 * D_FF), lambda i: (i // tiles, 0, 0)),
        ],
        compiler_params=_cparams(1),
        name="ffn_up",
    )(x, x, tr.mod, tr.mod, g_pre, w_up, conv_w, conv_b)


def _t5_bucket(dist):
    n = jnp.maximum(dist, 0)
    max_exact = N_BUCKETS // 2
    large = max_exact + (jnp.log(jnp.maximum(n, 1).astype(f32) / max_exact)
                         / math.log(MAX_DISTANCE / max_exact) * (N_BUCKETS - max_exact)).astype(jnp.int32)
    large = jnp.minimum(large, N_BUCKETS - 1)
    return jnp.where(n < max_exact, n, large)


def _bias_by_distance(rel_bias, n):
    return rel_bias[_t5_bucket(jnp.arange(n))]


ATT_T = 256


def _diff_attn_kernel(lam_ref, q_ref, k_ref, v_ref, bias_ref, sg_ref, o_ref, m_scr, l_scr, acc_scr, *, out_scale):
    qi = pl.program_id(1)
    kj = pl.program_id(2)
    t = q_ref.shape[0]

    @pl.when(kj == 0)
    def _():
        m_scr[...] = jnp.full_like(m_scr, NEG_BIG)
        l_scr[...] = jnp.zeros_like(l_scr)
        acc_scr[...] = jnp.zeros_like(acc_scr)

    @pl.when(kj <= qi)
    def _():
        row = lax.broadcasted_iota(jnp.int32, (t, t), 0) + (qi - kj) * t
        col = lax.broadcasted_iota(jnp.int32, (t, t), 1)
        causal = row >= col
        for hh in range(H_DIFF):
            vh = v_ref[:, hh * 2 * DH_DIFF:(hh + 1) * 2 * DH_DIFF]
            bias = bias_ref[hh]
            for half in range(2):
                n = 2 * hh + half
                sl = slice(n * DH_DIFF, (n + 1) * DH_DIFF)
                s = _dot_nt(q_ref[:, sl], k_ref[:, sl]) + bias
                s = jnp.where(causal, s, NEG_BIG)
                m_old = m_scr[n]
                m_new = jnp.maximum(m_old, jnp.max(s, axis=-1, keepdims=True))
                p = jnp.exp(s - m_new)
                alpha = jnp.exp(m_old - m_new)
                l_scr[n] = alpha * l_scr[n] + jnp.sum(p, axis=-1, keepdims=True)
                acc_scr[n] = alpha * acc_scr[n] + jnp.dot(p.astype(bf16), vh, preferred_element_type=f32)
                m_scr[n] = m_new

    @pl.when(kj == qi)
    def _():
        lam = lam_ref[0]
        for hh in range(H_DIFF):
            o = acc_scr[2 * hh] / l_scr[2 * hh] - lam * (acc_scr[2 * hh + 1] / l_scr[2 * hh + 1])
            ms = jnp.mean(o * o, axis=-1, keepdims=True)
            o = o * lax.rsqrt(ms + EPS) * sg_ref[...] * out_scale
            o_ref[:, hh * 2 * DH_DIFF:(hh + 1) * 2 * DH_DIFF] = o.astype(o_ref.dtype)


def _diff_attn(q, k, v, bias_tiles, lam, subln_g, lam_init, bsz, seq):
    t = ATT_T
    nq = seq // t
    qspec = pl.BlockSpec((t, MIX_W), lambda b, i, j: (b * nq + i, 0))
    kspec = pl.BlockSpec((t, MIX_W), lambda b, i, j: (b * nq + jnp.minimum(j, i), 0))
    return pl.pallas_call(
        functools.partial(_diff_attn_kernel, out_scale=1.0 - lam_init),
        out_shape=jax.ShapeDtypeStruct((bsz * seq, MIX_W), bf16),
        grid=(bsz, nq, nq),
        in_specs=[
            pl.BlockSpec(memory_space=pltpu.SMEM),
            qspec,
            kspec,
            kspec,
            pl.BlockSpec((H_DIFF, None, t, t), lambda b, i, j: (0, jnp.maximum(i - j, 0), 0, 0)),
            _const_spec((1, 2 * DH_DIFF), (0, 0)),
        ],
        out_specs=qspec,
        scratch_shapes=[
            pltpu.VMEM((2 * H_DIFF, t, 1), f32),
            pltpu.VMEM((2 * H_DIFF, t, 1), f32),
            pltpu.VMEM((2 * H_DIFF, t, 2 * DH_DIFF), f32),
        ],
        compiler_params=_cparams(3),
        name="diff_attn",
    )(lam, q, k, v, bias_tiles, subln_g)


def _log_sigmoid_pair(z):
    ls = jnp.minimum(z, 0.0) - jnp.log1p(jnp.exp(-jnp.abs(z)))
    return ls, ls - z


def _suffix_sum(lk, upper):
    hi, lo = _split(lk)
    return jnp.dot(hi, upper, preferred_element_type=f32) + jnp.dot(lo, upper, preferred_element_type=f32)


def _upper_ones(t):
    j = lax.broadcasted_iota(jnp.int32, (t, t), 0)
    k = lax.broadcasted_iota(jnp.int32, (t, t), 1)
    return jnp.where(j > k, 1.0, 0.0).astype(bf16)


def _sb_attn_kernel(q_ref, k_ref, v_ref, o_ref, r_scr, acc_scr):
    qi = pl.program_id(1)
    st = pl.program_id(2)
    t = q_ref.shape[0]

    @pl.when(st == 0)
    def _():
        r_scr[...] = jnp.zeros_like(r_scr)
        acc_scr[...] = jnp.zeros_like(acc_scr)

    @pl.when(st <= qi)
    def _():
        row = lax.broadcasted_iota(jnp.int32, (t, t), 0) + st * t
        col = lax.broadcasted_iota(jnp.int32, (t, t), 1)
        strict = col < row
        upper = _upper_ones(t)
        for hh in range(H_SB):
            sl = slice(hh * DH_SB, (hh + 1) * DH_SB)
            z = _dot_nt(q_ref[:, sl], k_ref[:, sl])
            ls, lk = _log_sigmoid_pair(z)
            lk = jnp.where(strict, lk, 0.0)
            after = _suffix_sum(lk, upper) + r_scr[hh]
            w = jnp.where(strict, jnp.exp(ls + after), 0.0)
            acc_scr[hh] = acc_scr[hh] + jnp.dot(w.astype(bf16), v_ref[:, sl], preferred_element_type=f32)
            r_scr[hh] = r_scr[hh] + jnp.sum(lk, axis=-1, keepdims=True)

    @pl.when(st == qi)
    def _():
        for hh in range(H_SB):
            o_ref[:, hh * DH_SB:(hh + 1) * DH_SB] = acc_scr[hh].astype(o_ref.dtype)


def _sb_attn(q, k, v, bsz, seq):
    t = ATT_T
    nq = seq // t
    qspec = pl.BlockSpec((t, MIX_W), lambda b, i, s: (b * nq + i, 0))
    kspec = pl.BlockSpec((t, MIX_W), lambda b, i, s: (b * nq + jnp.maximum(i - s, 0), 0))
    return pl.pallas_call(
        _sb_attn_kernel,
        out_shape=jax.ShapeDtypeStruct((bsz * seq, MIX_W), bf16),
        grid=(bsz, nq, nq),
        in_specs=[qspec, kspec, kspec],
        out_specs=qspec,
        scratch_shapes=[pltpu.VMEM((H_SB, t, 1), f32), pltpu.VMEM((H_SB, t, DH_SB), f32)],
        compiler_params=_cparams(3),
        name="sb_attn",
    )(q, k, v)


DEC_BB = 8


def _even_dec_kernel(q_ref, k_ref, v_ref, g_ref, u_ref, s0_ref, x0r_ref, x0i_ref, gn_ref, bw_ref, are_ref, aim_ref,
                     cw_ref, d_ref, glu_ref, o_ref, y_ref, s_ref, xr_ref, xi_ref, *, decay):
    pad = jnp.zeros((DK_RET - DEC_BB, DK_RET), f32)
    for hh in range(H_RET):
        sl = slice(hh * DK_RET, (hh + 1) * DK_RET)
        qh = q_ref[:, sl]
        kh = k_ref[:, sl]
        vh = v_ref[:, sl]
        q_t = jnp.concatenate([qh, pad], axis=0).T
        k_t = jnp.concatenate([kh, pad], axis=0).T
        qk = jnp.sum(qh * kh, axis=-1, keepdims=True)
        rows = []
        for b in range(DEC_BB):
            s0 = s0_ref[b, hh]
            qs = jnp.sum(q_t[:, b:b + 1] * s0, axis=0, keepdims=True)
            rows.append(qk[b:b + 1] * vh[b:b + 1] + qs * decay[hh])
            s_ref[b, hh] = s0 * decay[hh] + k_t[:, b:b + 1] * vh[b:b + 1]
        o = jnp.concatenate(rows, axis=0)
        mu = jnp.mean(o, axis=-1, keepdims=True)
        dlt = o - mu
        var = jnp.mean(dlt * dlt, axis=-1, keepdims=True)
        on = dlt * lax.rsqrt(var + EPS) * gn_ref[hh]
        o_ref[:, sl] = on * _silu(g_ref[:, sl])

    u = u_ref[...]
    bus = [_dot(_lhs(u[:, j * LANES:(j + 1) * LANES], 3), bw_ref[j], 3) for j in range(U_GROUPS)]
    bu_re = jnp.concatenate([b[:, :ST_PER_UG] for b in bus], axis=1)
    bu_im = jnp.concatenate([b[:, ST_PER_UG:] for b in bus], axis=1)
    are = are_ref[...]
    aim = aim_ref[...]
    x0r = x0r_ref[...]
    x0i = x0i_ref[...]
    xr = (are * x0r - aim * x0i) + bu_re
    xi = (are * x0i + aim * x0r) + bu_im
    xr_ref[...] = xr
    xi_ref[...] = xi
    ys = []
    for j in range(U_GROUPS):
        ss = slice(j * ST_PER_UG, (j + 1) * ST_PER_UG)
        ys.append(_dot(_lhs(xr[:, ss], 3), cw_ref[j, :ST_PER_UG, :], 3) + _dot(_lhs(xi[:, ss], 3), cw_ref[j, ST_PER_UG:, :], 3))
    y = _gelu(jnp.concatenate(ys, axis=1) + d_ref[...] * u)
    z = _dot(_lhs(y, 3), glu_ref[...], 3)
    y_ref[...] = y * _sigmoid(z)


def _even_dec(q, k, v, g, u, state_ret, s5_re, s5_im, ie, gn_g, bw, a_re, a_im, cw, d_skip, w_glu):
    nb = q.shape[0]
    row = pl.BlockSpec((DEC_BB, MIX_W), lambda i: (i, 0))
    st_spec = pl.BlockSpec((None, DEC_BB, H_RET, DK_RET, DV_RET), lambda i: (ie, i, 0, 0, 0))
    x_spec = pl.BlockSpec((None, DEC_BB, S5_N), lambda i: (ie, i, 0))
    decay = tuple(float(math.exp(math.log1p(-(2.0 ** (-5.0 - h))))) for h in range(H_RET))
    return pl.pallas_call(
        functools.partial(_even_dec_kernel, decay=decay),
        out_shape=[
            jax.ShapeDtypeStruct((nb, MIX_W), f32),
            jax.ShapeDtypeStruct((nb, S5_WIDTH), f32),
            jax.ShapeDtypeStruct((nb, H_RET, DK_RET, DV_RET), f32),
            jax.ShapeDtypeStruct((nb, S5_N), f32),
            jax.ShapeDtypeStruct((nb, S5_N), f32),
        ],
        grid=(nb // DEC_BB,),
        in_specs=[
            row, row, row, row, row, st_spec, x_spec, x_spec,
            _const_spec((H_RET, 1, DV_RET), (0, 0, 0)),
            _const_spec((U_GROUPS, LANES, 2 * ST_PER_UG), (0, 0, 0)),
            _const_spec((1, S5_N), (0, 0)),
            _const_spec((1, S5_N), (0, 0)),
            _const_spec((U_GROUPS, 2 * ST_PER_UG, LANES), (0, 0, 0)),
            _const_spec((1, S5_WIDTH), (0, 0)),
            _const_spec((S5_WIDTH, S5_WIDTH), (0, 0)),
        ],
        out_specs=[
            row, row,
            pl.BlockSpec((DEC_BB, H_RET, DK_RET, DV_RET), lambda i: (i, 0, 0, 0)),
            pl.BlockSpec((DEC_BB, S5_N), lambda i: (i, 0)),
            pl.BlockSpec((DEC_BB, S5_N), lambda i: (i, 0)),
        ],
        compiler_params=_cparams(1),
        name="even_dec",
    )(q, k, v, g, u, state_ret, s5_re, s5_im, gn_g.reshape(H_RET, 1, DV_RET), bw, a_re, a_im, cw, d_skip, w_glu)


def _ffn_up_dec_kernel(x_ref, sc_ref, sh_ref, g_ref, wv_ref, wg_ref, b0v_ref, b0g_ref, b1v_ref, b1g_ref,
                       cwv_ref, cwg_ref, cbv_ref, cbg_ref, act_ref, upv_ref, upg_ref):
    h = _lhs(_norm_mod(x_ref[...], g_ref[...], sc_ref[...], sh_ref[...]), 3)
    upv = _dot(h, wv_ref[...], 3)
    upg = _dot(h, wg_ref[...], 3)
    val = cwv_ref[0:1, :] * b0v_ref[...] + cwv_ref[1:2, :] * b1v_ref[...] + cwv_ref[2:3, :] * upv + cbv_ref[...]
    gate = cwg_ref[0:1, :] * b0g_ref[...] + cwg_ref[1:2, :] * b1g_ref[...] + cwg_ref[2:3, :] * upg + cbg_ref[...]
    act_ref[...] = val * _gelu(gate)
    upv_ref[...] = upv
    upg_ref[...] = upg


def _ffn_up_dec(x, mod_s, li, g_pre, w_up, conv_w, conv_b, state_conv2d):
    nb = x.shape[0]
    nch = D_FF // FF_CHUNK
    full = _const_spec((nb, D_MODEL), (0, 0))

    def mod(k):
        return pl.BlockSpec((None, nb, D_MODEL), lambda n: (li, 0, k))

    def cols(rows, off):
        return pl.BlockSpec((None, rows, FF_CHUNK), lambda n: (li, 0, n + off))

    out = pl.BlockSpec((nb, FF_CHUNK), lambda n: (0, n))
    return pl.pallas_call(
        _ffn_up_dec_kernel,
        out_shape=[jax.ShapeDtypeStruct((nb, D_FF), f32)] * 3,
        grid=(nch,),
        in_specs=[
            full, mod(4), mod(3), _const_spec((None, 1, D_MODEL), (li, 0, 0)),
            cols(D_MODEL, 0), cols(D_MODEL, nch),
            cols(nb, 0), cols(nb, nch), cols(nb, 2 * nch), cols(nb, 3 * nch),
            cols(CONV_W, 0), cols(CONV_W, nch), cols(1, 0), cols(1, nch),
        ],
        out_specs=[out, out, out],
        compiler_params=_cparams(1),
        name="ffn_up_dec",
    )(x, mod_s, mod_s, g_pre, w_up, w_up, state_conv2d, state_conv2d, state_conv2d, state_conv2d,
      conv_w, conv_w, conv_b, conv_b)


DEC_PP = 8


def _attn_dec_kernel(pt_ref, lam_ref, qd_ref, qs_ref, kn_ref, vn_ref, bias_ref, bias0_ref, sg_ref, *rest, n_pages, out_scale):
    del pt_ref
    pages = rest[:4 * DEC_PP]
    o_ref, md, ld, accd, rs, accs = rest[4 * DEC_PP:]
    j = pl.program_id(1)
    rows = H_SB
    head_of_lane = lax.broadcasted_iota(jnp.int32, (rows, MIX_W), 1) // DH_SB
    own = head_of_lane == lax.broadcasted_iota(jnp.int32, (rows, MIX_W), 0)
    q_diff = jnp.where(own, qd_ref[...], 0.0)
    q_sb = jnp.where(own, qs_ref[...], 0.0)

    @pl.when(j == 0)
    def _():
        md[...] = jnp.sum(q_diff * kn_ref[...], axis=-1, keepdims=True) + bias0_ref[...]
        ld[...] = jnp.ones_like(ld)
        accd[...] = jnp.broadcast_to(vn_ref[...], accd.shape)
        rs[...] = jnp.zeros_like(rs)
        accs[...] = jnp.zeros_like(accs)

    qd_b = q_diff.astype(bf16)
    qs_b = q_sb.astype(bf16)
    upper = _upper_ones(PAGE_SIZE)
    for i in range(DEC_PP):
        page = n_pages - 1 - (j * DEC_PP + i)
        dk_ref, dv_ref, sk_ref, sv_ref = pages[4 * i:4 * i + 4]
        s = _dot_nt(qd_b, dk_ref[...].astype(bf16)) + bias_ref[page]
        m_old = md[...]
        m_new = jnp.maximum(m_old, jnp.max(s, axis=-1, keepdims=True))
        p = jnp.exp(s - m_new)
        alpha = jnp.exp(m_old - m_new)
        ld[...] = alpha * ld[...] + jnp.sum(p, axis=-1, keepdims=True)
        accd[...] = alpha * accd[...] + jnp.dot(p.astype(bf16), dv_ref[...].astype(bf16), preferred_element_type=f32)
        md[...] = m_new

        z = _dot_nt(qs_b, sk_ref[...].astype(bf16))
        ls, lk = _log_sigmoid_pair(z)
        after = _suffix_sum(lk, upper) + rs[...]
        w = jnp.exp(ls + after)
        accs[...] = accs[...] + jnp.dot(w.astype(bf16), sv_ref[...].astype(bf16), preferred_element_type=f32)
        rs[...] = rs[...] + jnp.sum(lk, axis=-1, keepdims=True)

    @pl.when(j == pl.num_programs(1) - 1)
    def _():
        lam = lam_ref[0]
        acc = accd[...] / ld[...]
        pieces = []
        for hh in range(H_DIFF):
            sl = slice(hh * 2 * DH_DIFF, (hh + 1) * 2 * DH_DIFF)
            o = acc[2 * hh:2 * hh + 1, sl] - lam * acc[2 * hh + 1:2 * hh + 2, sl]
            ms = jnp.mean(o * o, axis=-1, keepdims=True)
            pieces.append(o * lax.rsqrt(ms + EPS) * sg_ref[...] * out_scale)
        a_sb = accs[...]
        for hh in range(H_SB):
            pieces.append(a_sb[hh:hh + 1, hh * DH_SB:(hh + 1) * DH_SB])
        o_ref[...] = jnp.concatenate(pieces, axis=1)


def _attn_dec(page_table, lam, qd, qs, k_new, v_new, bias_pages, bias0, subln_g, caches, io, lam_init):
    nb, n_pages = page_table.shape
    row3 = lambda a: a.reshape(nb, 1, a.shape[-1])
    rspec = pl.BlockSpec((None, 1, MIX_W), lambda b, j, pt: (b, 0, 0))

    def page_spec(i):
        return pl.BlockSpec((None, None, PAGE_SIZE, MIX_W),
                            lambda b, j, pt: (io, pt[b, n_pages - 1 - (j * DEC_PP + i)], 0, 0))

    page_specs, page_args = [], []
    for i in range(DEC_PP):
        for cache in caches:
            page_specs.append(page_spec(i))
            page_args.append(cache)
    grid_spec = pltpu.PrefetchScalarGridSpec(
        num_scalar_prefetch=1,
        grid=(nb, n_pages // DEC_PP),
        in_specs=[
            pl.BlockSpec(memory_space=pltpu.SMEM),
            rspec, rspec, rspec, rspec,
            pl.BlockSpec((n_pages, H_SB, PAGE_SIZE), lambda b, j, pt: (0, 0, 0)),
            pl.BlockSpec((H_SB, 1), lambda b, j, pt: (0, 0)),
            pl.BlockSpec((1, 2 * DH_DIFF), lambda b, j, pt: (0, 0)),
        ] + page_specs,
        out_specs=pl.BlockSpec((None, 1, 2 * MIX_W), lambda b, j, pt: (b, 0, 0)),
        scratch_shapes=[
            pltpu.VMEM((H_SB, 1), f32), pltpu.VMEM((H_SB, 1), f32), pltpu.VMEM((H_SB, MIX_W), f32),
            pltpu.VMEM((H_SB, 1), f32), pltpu.VMEM((H_SB, MIX_W), f32),
        ],
    )
    out = pl.pallas_call(
        functools.partial(_attn_dec_kernel, n_pages=n_pages, out_scale=1.0 - lam_init),
        out_shape=jax.ShapeDtypeStruct((nb, 1, 2 * MIX_W), f32),
        grid_spec=grid_spec,
        compiler_params=_cparams(2),
        name="attn_dec",
    )(page_table, lam, row3(qd), row3(qs), row3(k_new), row3(v_new), bias_pages, bias0, subln_g, *page_args)
    return out.reshape(nb, 2 * MIX_W)


def _lam_init(li):
    return 0.8 - 0.6 * math.exp(-0.3 * li)


def _diff_lambda(lam_vecs, li):
    lv = lam_vecs.astype(f32)
    lam = jnp.exp(jnp.dot(lv[0], lv[1])) - jnp.exp(jnp.dot(lv[2], lv[3])) + _lam_init(li)
    return lam.reshape(1)


def _rope_tables(pos):
    half = DK_RET // 2
    inv = jnp.power(ROPE_BASE, -jnp.arange(half, dtype=f32) / half)
    ang = pos.astype(f32)[:, None] * inv[None, :]
    cos, sin = jnp.cos(ang), jnp.sin(ang)
    return jnp.concatenate([cos, cos], axis=-1), jnp.concatenate([-sin, sin], axis=-1)


def _prompt_bias_tiles(bias_d, seq):
    t = ATT_T
    nd = seq // t
    r = jnp.arange(t)
    dist = jnp.arange(nd)[:, None, None] * t + r[None, :, None] - r[None, None, :]
    return jnp.transpose(bias_d[jnp.clip(dist, 0, seq - 1)], (3, 0, 1, 2))


def _prompt_trunk(x_prompt, mod_p, prm):
    bsz, seq, _ = x_prompt.shape
    tm = min(512, seq)
    tr = _Trunk(mod_p, seq, tm, 1)
    x = x_prompt.reshape(bsz * seq, D_MODEL)
    cos2, sin2 = _rope_tables(jnp.arange(seq))
    bias_tiles = _prompt_bias_tiles(_bias_by_distance(prm["rel_bias"], seq), seq)
    new = {k: [] for k in ("ret", "s5r", "s5i", "kd", "vd", "ks", "vs", "conv")}
    for li in range(DEPTH):
        if li % 2 == 0:
            ie = li // 2
            s5p = prm["s5"][ie]
            q, k, v, g, u = _proj_even(tr, x, li, prm["g_pre_mix"], prm["w_in_even_b"], cos2, sin2, (bf16, bf16, bf16, f32, f32))
            o, ret_s = _retention(q, k, v, g, prm["ret_gn_g"][ie], bsz, seq)
            y, xr, xi = _s5(u, s5p["bw"].astype(bf16), s5p["tabs"], s5p["cw"].astype(bf16), s5p["d"], s5p["glu_b"], bsz, seq,
                            min(256, seq))
            new["ret"].append(ret_s)
            new["s5r"].append(xr)
            new["s5i"].append(xi)
            x = _outproj(tr, [o, y], prm["w_out_even_b"], ie, x, li, 2, prm["g_post_mix"])
        else:
            io = li // 2
            dq, dk, dv, sq, sk, sv, dk_b, dv_b, sk_b, sv_b = _proj_odd(tr, x, li, prm["g_pre_mix"], prm["w_in_odd_b"], True)
            lam = _diff_lambda(prm["diff_lam"][io], li)
            od = _diff_attn(dq, dk_b, dv_b, bias_tiles, lam, prm["diff_subln_g"][io].reshape(1, -1), _lam_init(li), bsz, seq)
            osb = _sb_attn(sq, sk_b, sv_b, bsz, seq)
            new["kd"].append(dk.reshape(bsz, seq, H_DIFF, 2 * DH_DIFF))
            new["vd"].append(dv.reshape(bsz, seq, H_DIFF, 2 * DH_DIFF))
            new["ks"].append(sk.reshape(bsz, seq, H_SB, DH_SB))
            new["vs"].append(sv.reshape(bsz, seq, H_SB, DH_SB))
            x = _outproj(tr, [od, osb], prm["w_out_odd_b"], io, x, li, 2, prm["g_post_mix"])
        act, cs = _ffn_up(tr, x, li, prm["g_pre_ffn"], prm["w_up_b"], prm["conv_w"], prm["conv_b"], bsz)
        new["conv"].append(cs)
        x = _outproj(tr, [act], prm["w_down_b"], li, x, li, 5, prm["g_post_ffn"])
    st = lambda t: jnp.stack(t, axis=0)
    return x.reshape(bsz, seq, D_MODEL), tuple(st(new[k]) for k in ("ret", "s5r", "s5i", "kd", "vd", "ks", "vs", "conv"))


def _sample_trunk(x_sample, mod_s, prm, past):
    nb = x_sample.shape[0]
    tr = _Trunk(mod_s, 1, nb, 3)
    x = x_sample.reshape(nb, D_MODEL)
    page_table = past["page_table"]
    n_pages = page_table.shape[1]
    q_off = n_pages * PAGE_SIZE
    cos2, sin2 = _rope_tables(jnp.full((1,), q_off))
    bias_d = _bias_by_distance(prm["rel_bias"], q_off + 1)
    key_pos = jnp.arange(q_off).reshape(n_pages, PAGE_SIZE)
    bias_pages = jnp.repeat(jnp.transpose(bias_d[q_off - key_pos], (0, 2, 1)), 2, axis=1)
    bias0 = jnp.repeat(bias_d[0], 2).reshape(2 * H_DIFF, 1)
    s5_re = past["s5_re"].reshape(past["s5_re"].shape[0], nb, S5_N)
    s5_im = past["s5_im"].reshape(past["s5_im"].shape[0], nb, S5_N)
    conv2d = past["conv"].reshape(DEPTH, nb, (CONV_W - 1) * 2 * D_FF)
    caches = tuple(past[k].reshape(past[k].shape[0], past[k].shape[1], PAGE_SIZE, MIX_W)
                   for k in ("diff_k", "diff_v", "sb_k", "sb_v"))
    new = {k: [] for k in ("ret", "s5r", "s5i", "kd", "vd", "ks", "vs", "conv")}
    for li in range(DEPTH):
        if li % 2 == 0:
            ie = li // 2
            s5p = prm["s5"][ie]
            q, k, v, g, u = _proj_even(tr, x, li, prm["g_pre_mix"], prm["w_in_even"], cos2, sin2, (f32,) * 5)
            o, y, ret_s, xr, xi = _even_dec(q, k, v, g, u, past["ret"], s5_re, s5_im, ie, prm["ret_gn_g"][ie],
                                            s5p["bw"], s5p["a_re"], s5p["a_im"], s5p["cw"], s5p["d"], s5p["glu"])
            new["ret"].append(ret_s)
            new["s5r"].append(xr.reshape(nb, S5_GROUPS, S5_STATE))
            new["s5i"].append(xi.reshape(nb, S5_GROUPS, S5_STATE))
            x = _outproj(tr, [o, y], prm["w_out_even"], ie, x, li, 2, prm["g_post_mix"])
        else:
            io = li // 2
            dq, dk, dv, sq, sk, sv = _proj_odd(tr, x, li, prm["g_pre_mix"], prm["w_in_odd"], False)
            lam = _diff_lambda(prm["diff_lam"][io], li)
            mix = _attn_dec(page_table, lam, dq, sq, dk, dv, bias_pages, bias0, prm["diff_subln_g"][io].reshape(1, -1),
                            caches, io, _lam_init(li))
            new["kd"].append(dk.reshape(nb, 1, H_DIFF, 2 * DH_DIFF))
            new["vd"].append(dv.reshape(nb, 1, H_DIFF, 2 * DH_DIFF))
            new["ks"].append(sk.reshape(nb, 1, H_SB, DH_SB))
            new["vs"].append(sv.reshape(nb, 1, H_SB, DH_SB))
            x = _outproj(tr, [mix[:, :MIX_W], mix[:, MIX_W:]], prm["w_out_odd"], io, x, li, 2, prm["g_post_mix"])
        act, upv, upg = _ffn_up_dec(x, mod_s, li, prm["g_pre_ffn"], prm["w_up"], prm["conv_w"], prm["conv_b"], conv2d)
        up = jnp.concatenate([upv, upg], axis=-1)
        new["conv"].append(jnp.stack([past["conv"][li][:, CONV_W - 2], up], axis=1))
        x = _outproj(tr, [act], prm["w_down"], li, x, li, 5, prm["g_post_ffn"])
    st = lambda t: jnp.stack(t, axis=0)
    return x.reshape(nb, 1, D_MODEL), tuple(st(new[k]) for k in ("ret", "s5r", "s5i", "kd", "vd", "ks", "vs", "conv"))


def _prepare_params(g_pre_mix, g_post_mix, g_pre_ffn, g_post_ffn, w_in_even, w_out_even, ret_gn_g, s5_lam_re, s5_lam_im,
                    s5_log_step, s5_b_re, s5_b_im, s5_c_re, s5_c_im, s5_d, s5_w_glu, w_in_odd, w_out_odd, diff_lam,
                    diff_subln_g, rel_bias, w_up, conv_w, conv_b, w_down):
    g3 = lambda g: g.reshape(g.shape[0], 1, g.shape[1])
    prm = dict(
        g_pre_mix=g3(g_pre_mix), g_post_mix=g3(g_post_mix), g_pre_ffn=g3(g_pre_ffn), g_post_ffn=g3(g_post_ffn),
        w_in_even=w_in_even, w_out_even=w_out_even, w_in_odd=w_in_odd, w_out_odd=w_out_odd, w_up=w_up, w_down=w_down,
        w_in_even_b=w_in_even.astype(bf16), w_out_even_b=w_out_even.astype(bf16), w_in_odd_b=w_in_odd.astype(bf16),
        w_out_odd_b=w_out_odd.astype(bf16), w_up_b=w_up.astype(bf16), w_down_b=w_down.astype(bf16),
        ret_gn_g=ret_gn_g, diff_lam=diff_lam, diff_subln_g=diff_subln_g, rel_bias=rel_bias,
        conv_w=conv_w, conv_b=g3(conv_b),
    )
    s5 = []
    for ie in range(s5_lam_re.shape[0]):
        a_re, a_im, bw, cw = _s5_params(s5_lam_re[ie], s5_lam_im[ie], s5_log_step[ie], s5_b_re[ie], s5_b_im[ie],
                                        s5_c_re[ie], s5_c_im[ie])
        s5.append(dict(a_re=a_re, a_im=a_im, bw=bw, cw=cw, tabs=_s5_scan_tables(a_re, a_im),
                       d=s5_d[ie].reshape(1, S5_WIDTH), glu=s5_w_glu[ie], glu_b=s5_w_glu[ie].astype(bf16)))
    prm["s5"] = s5
    return prm


def kernel(x_prompt, x_sample, c_prompt, c_sample, state_ret, state_s5_re, state_s5_im, cache_diff_k, cache_diff_v, cache_sb_k, cache_sb_v, state_conv, page_table, w_ada, b_ada, g_pre_mix, g_post_mix, g_pre_ffn, g_post_ffn, w_in_even, w_out_even, ret_gn_g, s5_lam_re, s5_lam_im, s5_log_step, s5_b_re, s5_b_im, s5_c_re, s5_c_im, s5_d, s5_w_glu, w_in_odd, w_out_odd, diff_lam, diff_subln_g, rel_bias, w_up, conv_w, conv_b, w_down):
    nb = x_sample.shape[0]
    bsz = x_prompt.shape[0]
    prm = _prepare_params(g_pre_mix, g_post_mix, g_pre_ffn, g_post_ffn, w_in_even, w_out_even, ret_gn_g, s5_lam_re,
                          s5_lam_im, s5_log_step, s5_b_re, s5_b_im, s5_c_re, s5_c_im, s5_d, s5_w_glu, w_in_odd, w_out_odd,
                          diff_lam, diff_subln_g, rel_bias, w_up, conv_w, conv_b, w_down)
    mod = _ada(jnp.concatenate([c_sample, c_prompt], axis=0), w_ada, b_ada)
    mod_s = mod[:, :nb]
    mod_p = mod[:, nb:].reshape(DEPTH, bsz, 1, 6 * D_MODEL)
    past = dict(ret=state_ret, s5_re=state_s5_re, s5_im=state_s5_im, diff_k=cache_diff_k, diff_v=cache_diff_v,
                sb_k=cache_sb_k, sb_v=cache_sb_v, conv=state_conv, page_table=page_table)
    y_prompt, sp = _prompt_trunk(x_prompt, mod_p, prm)
    y_sample, ss = _sample_trunk(x_sample, mod_s, prm, past)
    ret_p, s5r_p, s5i_p, kd_p, vd_p, ks_p, vs_p, conv_p = sp
    ret_s, s5r_s, s5i_s, kd_s, vd_s, ks_s, vs_s, conv_s = ss
    return (y_prompt, y_sample, ret_p, ret_s, s5r_p, s5r_s, s5i_p, s5i_s, kd_p, kd_s, vd_p, vd_s,
            ks_p, ks_s, vs_p, vs_s, conv_p, conv_s)
```

```python
import functools
import math

import jax
import jax.numpy as jnp
from jax import lax
from jax.experimental import pallas as pl
from jax.experimental.pallas import tpu as pltpu

f32 = jnp.float32
bf16 = jnp.bfloat16

D_MODEL = 1024
DEPTH = 4
PAST_LEN = 8192
PAGE_SIZE = 128
H_RET = 4
DK_RET = 128
DV_RET = 128
RET_CHUNK = 128
ROPE_BASE = 10000.0
S5_WIDTH = D_MODEL // 2
S5_GROUP = 16
S5_GROUPS = S5_WIDTH // S5_GROUP
S5_STATE = 64
S5_N = S5_GROUPS * S5_STATE
H_DIFF = 4
DH_DIFF = 64
H_SB = 8
DH_SB = 64
N_BUCKETS = 32
MAX_DISTANCE = 128
D_FF = 2816
CONV_W = 3
EPS = 1e-6
MIX_W = 512

LANES = 128
SUBLANES = 8
VMEM_LIMIT = 56 * 1024 * 1024
NEG_BIG = -1e30


def _cparams(n_axes):
    return pltpu.CompilerParams(dimension_semantics=("arbitrary",) * n_axes, vmem_limit_bytes=VMEM_LIMIT)


def _split(a):
    hi = a.astype(bf16)
    lo = (a - hi.astype(f32)).astype(bf16)
    return hi, lo


def _lhs(a, passes):
    if passes == 1:
        return (a.astype(bf16),)
    return _split(a.astype(f32))


def _dot(lhs, b, passes):
    if passes == 1:
        return jnp.dot(lhs[0], b.astype(bf16), preferred_element_type=f32)
    b_hi, b_lo = _split(b.astype(f32))
    return jnp.dot(lhs[0], b_hi, preferred_element_type=f32) + (
        jnp.dot(lhs[0], b_lo, preferred_element_type=f32) + jnp.dot(lhs[1], b_hi, preferred_element_type=f32)
    )


def _dot_nt(a, b):
    return lax.dot_general(a, b, (((1,), (1,)), ((), ())), preferred_element_type=f32)


def _norm_mod(x, g, sc, sh):
    ms = jnp.mean(x * x, axis=-1, keepdims=True)
    return (x * lax.rsqrt(ms + EPS) * g) * (1.0 + sc) + sh


def _gelu(x):
    return 0.5 * x * (1.0 + jnp.tanh(math.sqrt(2.0 / math.pi) * (x + 0.044715 * (x * x * x))))


def _sigmoid(x):
    return 1.0 / (1.0 + jnp.exp(-x))


def _silu(x):
    return x * _sigmoid(x)


def _ada_kernel(c_ref, w_ref, b_ref, o_ref):
    s = _silu(c_ref[...])
    o_ref[...] = _dot(_lhs(s, 3), w_ref[...], 3) + b_ref[...]


def _ada(c_all, w_ada, b_ada):
    n = c_all.shape[0]
    tn = 1024
    return pl.pallas_call(
        _ada_kernel,
        out_shape=jax.ShapeDtypeStruct((DEPTH, n, 6 * D_MODEL), f32),
        grid=(DEPTH, 6 * D_MODEL // tn),
        in_specs=[
            pl.BlockSpec((n, D_MODEL), lambda l, j: (0, 0)),
            pl.BlockSpec((None, D_MODEL, tn), lambda l, j: (l, 0, j)),
            pl.BlockSpec((None, 1, tn), lambda l, j: (l, 0, j)),
        ],
        out_specs=pl.BlockSpec((None, n, tn), lambda l, j: (l, 0, j)),
        compiler_params=_cparams(2),
        name="ada_mod",
    )(c_all, w_ada, b_ada.reshape(DEPTH, 1, 6 * D_MODEL))


class _Trunk:
    def __init__(self, mod, seq, tm, passes):
        self.mod = mod
        self.seq = seq
        self.tm = tm
        self.passes = passes
        self.per_row = seq == 1

    def mod_spec(self, li, k):
        if self.per_row:
            return pl.BlockSpec((None, self.tm, D_MODEL), lambda i: (li, i, k))
        tiles_per_seq = self.seq // self.tm
        return pl.BlockSpec((None, None, 1, D_MODEL), lambda i: (li, i // tiles_per_seq, 0, k))

    def row_spec(self, width):
        return pl.BlockSpec((self.tm, width), lambda i: (i, 0))


def _const_spec(shape, idx):
    return pl.BlockSpec(shape, lambda *_: idx)


def _rope(a, cos2, sin2):
    return a * cos2 + pltpu.roll(a, DK_RET // 2, 1) * sin2


def _proj_even_kernel(x_ref, sc_ref, sh_ref, g_ref, w_ref, cos_ref, sin_ref, q_ref, k_ref, v_ref, gg_ref, u_ref, *, passes):
    h = _lhs(_norm_mod(x_ref[...], g_ref[...], sc_ref[...], sh_ref[...]), passes)
    cos2 = cos_ref[...]
    sin2 = sin_ref[...]
    outs = (q_ref, k_ref, v_ref, gg_ref, u_ref)
    for n, ref in enumerate(outs):
        acc = _dot(h, w_ref[:, n * MIX_W:(n + 1) * MIX_W], passes)
        if n < 2:
            scale = 1.0 if n == 0 else DK_RET ** -0.5
            for hh in range(H_RET):
                sl = slice(hh * DK_RET, (hh + 1) * DK_RET)
                ref[:, sl] = (_rope(acc[:, sl], cos2, sin2) * scale).astype(ref.dtype)
        else:
            ref[...] = acc.astype(ref.dtype)


def _proj_even(tr, x, li, g_pre, w_in, cos2, sin2, out_dtypes):
    t = x.shape[0]
    tm = tr.tm
    n_in = w_in.shape[-1]
    if tr.per_row:
        cs_spec = pl.BlockSpec((1, DK_RET), lambda i: (0, 0))
    else:
        tiles = tr.seq // tm
        cs_spec = pl.BlockSpec((tm, DK_RET), lambda i: (i % tiles, 0))
    return pl.pallas_call(
        functools.partial(_proj_even_kernel, passes=tr.passes),
        out_shape=[jax.ShapeDtypeStruct((t, MIX_W), dt) for dt in out_dtypes],
        grid=(t // tm,),
        in_specs=[
            tr.row_spec(D_MODEL),
            tr.mod_spec(li, 1),
            tr.mod_spec(li, 0),
            _const_spec((None, 1, D_MODEL), (li, 0, 0)),
            _const_spec((None, D_MODEL, n_in), (li // 2, 0, 0)),
            cs_spec,
            cs_spec,
        ],
        out_specs=[tr.row_spec(MIX_W) for _ in out_dtypes],
        compiler_params=_cparams(1),
        name="proj_even",
    )(x, tr.mod, tr.mod, g_pre, w_in, cos2, sin2)


def _proj_odd_kernel(x_ref, sc_ref, sh_ref, g_ref, w_ref, *out_refs, passes, with_copies):
    h = _lhs(_norm_mod(x_ref[...], g_ref[...], sc_ref[...], sh_ref[...]), passes)
    for n in range(6):
        acc = _dot(h, w_ref[:, n * MIX_W:(n + 1) * MIX_W], passes)
        if n in (0, 3):
            acc = acc * (DH_DIFF ** -0.5 if n == 0 else DH_SB ** -0.5)
        out_refs[n][...] = acc.astype(out_refs[n].dtype)
        if with_copies and n not in (0, 3):
            out_refs[6 + (n if n < 3 else n - 1) - 1][...] = acc.astype(bf16)


def _proj_odd(tr, x, li, g_pre, w_in, with_copies):
    t = x.shape[0]
    q_dt = bf16 if with_copies else f32
    dts = [q_dt, f32, f32, q_dt, f32, f32] + ([bf16] * 4 if with_copies else [])
    return pl.pallas_call(
        functools.partial(_proj_odd_kernel, passes=tr.passes, with_copies=with_copies),
        out_shape=[jax.ShapeDtypeStruct((t, MIX_W), dt) for dt in dts],
        grid=(t // tr.tm,),
        in_specs=[
            tr.row_spec(D_MODEL),
            tr.mod_spec(li, 1),
            tr.mod_spec(li, 0),
            _const_spec((None, 1, D_MODEL), (li, 0, 0)),
            _const_spec((None, D_MODEL, 6 * MIX_W), (li // 2, 0, 0)),
        ],
        out_specs=[tr.row_spec(MIX_W) for _ in dts],
        compiler_params=_cparams(1),
        name="proj_odd",
    )(x, tr.mod, tr.mod, g_pre, w_in)


def _outproj_kernel(*refs, n_in, passes):
    a_refs = refs[:n_in]
    w_refs = refs[n_in:2 * n_in]
    x_ref, gate_ref, gp_ref, o_ref = refs[2 * n_in:]
    acc = _dot(_lhs(a_refs[0][...], passes), w_refs[0][...], passes)
    for a_ref, w_ref in zip(a_refs[1:], w_refs[1:]):
        acc = acc + _dot(_lhs(a_ref[...], passes), w_ref[...], passes)
    ms = jnp.mean(acc * acc, axis=-1, keepdims=True)
    y = acc * lax.rsqrt(ms + EPS) * gp_ref[...]
    o_ref[...] = x_ref[...] + gate_ref[...] * y


def _outproj(tr, acts, w, w_idx, x, li, gate_k, g_post):
    t = x.shape[0]
    widths = [a.shape[1] for a in acts]
    assert all(wd == widths[0] for wd in widths)
    w_specs = [pl.BlockSpec((None, widths[0], D_MODEL), lambda i, n=n: (w_idx, n, 0)) for n in range(len(acts))]
    return pl.pallas_call(
        functools.partial(_outproj_kernel, n_in=len(acts), passes=tr.passes),
        out_shape=jax.ShapeDtypeStruct((t, D_MODEL), f32),
        grid=(t // tr.tm,),
        in_specs=[tr.row_spec(wd) for wd in widths] + w_specs + [
            tr.row_spec(D_MODEL),
            tr.mod_spec(li, gate_k),
            _const_spec((None, 1, D_MODEL), (li, 0, 0)),
        ],
        out_specs=tr.row_spec(D_MODEL),
        compiler_params=_cparams(1),
        name="outproj",
    )(*acts, *([w] * len(acts)), x, tr.mod, g_post)


def _retention_kernel(q_ref, k_ref, v_ref, g_ref, intra_ref, qd_ref, kd_ref, gn_ref, o_ref, s_out_ref, s_scr, *, chunk_decay):
    c = pl.program_id(1)

    @pl.when(c == 0)
    def _():
        s_scr[...] = jnp.zeros_like(s_scr)

    for hh in range(H_RET):
        sl = slice(hh * DK_RET, (hh + 1) * DK_RET)
        qh = q_ref[:, sl]
        kh = k_ref[:, sl]
        vh = v_ref[:, sl]
        s_old = s_scr[hh]
        scores = _dot_nt(qh, kh) * intra_ref[hh]
        o = jnp.dot(scores.astype(bf16), vh, preferred_element_type=f32)
        o = o + jnp.dot(qh, s_old.astype(bf16), preferred_element_type=f32) * qd_ref[hh]
        kt = (kh.astype(f32) * kd_ref[hh]).T.astype(bf16)
        s_scr[hh] = s_old * chunk_decay[hh] + jnp.dot(kt, vh, preferred_element_type=f32)
        mu = jnp.mean(o, axis=-1, keepdims=True)
        d = o - mu
        var = jnp.mean(d * d, axis=-1, keepdims=True)
        on = d * lax.rsqrt(var + EPS) * gn_ref[hh]
        o_ref[:, sl] = (on * _silu(g_ref[:, sl])).astype(o_ref.dtype)

    @pl.when(c == pl.num_programs(1) - 1)
    def _():
        s_out_ref[...] = s_scr[...]


def _retention_tables(c):
    log_g = jnp.log1p(-jnp.exp2(-5.0 - jnp.arange(H_RET, dtype=f32)))
    idx = jnp.arange(c, dtype=f32)
    diff = idx[:, None] - idx[None, :]
    intra = jnp.where(diff >= 0, jnp.exp(jnp.maximum(diff, 0.0)[None] * log_g[:, None, None]), 0.0)
    q_decay = jnp.exp((idx + 1.0)[None, :] * log_g[:, None])[..., None]
    k_decay = jnp.exp((c - 1.0 - idx)[None, :] * log_g[:, None])[..., None]
    qd = jnp.broadcast_to(q_decay, (H_RET, c, DV_RET))
    kd = jnp.broadcast_to(k_decay, (H_RET, c, DK_RET))
    return intra, qd, kd


def _chunk_decay(c):
    return tuple(float(math.exp(c * math.log1p(-(2.0 ** (-5.0 - h))))) for h in range(H_RET))


def _retention(q, k, v, g, gn_g, bsz, seq):
    c = RET_CHUNK
    nc = seq // c
    intra, qd, kd = _retention_tables(c)
    row = pl.BlockSpec((c, MIX_W), lambda b, j: (b * nc + j, 0))
    tab = _const_spec((H_RET, c, c), (0, 0, 0))
    return pl.pallas_call(
        functools.partial(_retention_kernel, chunk_decay=_chunk_decay(c)),
        out_shape=[
            jax.ShapeDtypeStruct((bsz * seq, MIX_W), bf16),
            jax.ShapeDtypeStruct((bsz, H_RET, DK_RET, DV_RET), f32),
        ],
        grid=(bsz, nc),
        in_specs=[row, row, row, row, tab, tab, tab, _const_spec((H_RET, 1, DV_RET), (0, 0, 0))],
        out_specs=[row, pl.BlockSpec((None, H_RET, DK_RET, DV_RET), lambda b, j: (b, 0, 0, 0))],
        scratch_shapes=[pltpu.VMEM((H_RET, DK_RET, DV_RET), f32)],
        compiler_params=_cparams(2),
        name="retention",
    )(q, k, v, g, intra, qd, kd, gn_g.reshape(H_RET, 1, DV_RET))


SCAN_LW = 512
U_GROUPS = S5_WIDTH // LANES
ST_PER_UG = S5_N // U_GROUPS


def _s5_params(lam_re, lam_im, log_step, b_re, b_im, c_re, c_im):
    dt = jnp.exp(log_step)[:, None]
    mag = jnp.exp(lam_re * dt)
    a_re, a_im = mag * jnp.cos(lam_im * dt), mag * jnp.sin(lam_im * dt)
    den = lam_re * lam_re + lam_im * lam_im
    coef_re = ((a_re - 1.0) * lam_re + a_im * lam_im) / den
    coef_im = (a_im * lam_re - (a_re - 1.0) * lam_im) / den
    bbar_re = coef_re[..., None] * b_re - coef_im[..., None] * b_im
    bbar_im = coef_re[..., None] * b_im + coef_im[..., None] * b_re
    gpb = LANES // S5_GROUP
    eye = jnp.eye(gpb, dtype=f32)

    def b_layout(bb):
        bb = bb.reshape(U_GROUPS, gpb, S5_STATE, S5_GROUP).transpose(0, 1, 3, 2)
        return jnp.einsum("jgip,gh->jgihp", bb, eye).reshape(U_GROUPS, LANES, ST_PER_UG)

    def c_layout(cc):
        cc = cc.reshape(U_GROUPS, gpb, S5_GROUP, S5_STATE)
        return jnp.einsum("jgip,gh->jgphi", cc, eye).reshape(U_GROUPS, ST_PER_UG, LANES)

    bw = jnp.concatenate([b_layout(bbar_re), b_layout(bbar_im)], axis=-1)
    cw = jnp.concatenate([c_layout(c_re), -c_layout(c_im)], axis=1)
    return a_re.reshape(1, S5_N), a_im.reshape(1, S5_N), bw, cw


def _cmul(ar, ai, br, bi):
    return ar * br - ai * bi, ar * bi + ai * br


def _s5_scan_tables(a_re, a_im):
    rows = jnp.arange(SUBLANES)[:, None]
    tabs = []
    pr, pi = a_re, a_im
    pows = {}
    cr, ci = a_re, a_im
    carry_r, carry_i = [cr], [ci]
    for _ in range(SUBLANES - 1):
        cr, ci = _cmul(cr, ci, a_re, a_im)
        carry_r.append(cr)
        carry_i.append(ci)
    for d in (1, 2, 4):
        pows[d] = (carry_r[d - 1], carry_i[d - 1])
    for d in (1, 2, 4):
        tabs.append(jnp.where(rows >= d, pows[d][0], 0.0))
        tabs.append(jnp.where(rows >= d, pows[d][1], 0.0))
    tabs.append(jnp.concatenate(carry_r, axis=0))
    tabs.append(jnp.concatenate(carry_i, axis=0))
    del pr, pi
    return jnp.stack(tabs, axis=0)


def _s5_kernel(u_ref, bw_ref, tab_ref, cw_ref, d_ref, glu_ref, y_ref, xr_out, xi_out, xr_scr, xi_scr, car_scr, *, ts):
    t = pl.program_id(1)

    @pl.when(t == 0)
    def _():
        car_scr[...] = jnp.zeros_like(car_scr)

    u = u_ref[...]
    ub = u.astype(bf16)
    for j in range(U_GROUPS):
        bu = jnp.dot(ub[:, j * LANES:(j + 1) * LANES], bw_ref[j], preferred_element_type=f32)
        xr_scr[:, j * ST_PER_UG:(j + 1) * ST_PER_UG] = bu[:, :ST_PER_UG]
        xi_scr[:, j * ST_PER_UG:(j + 1) * ST_PER_UG] = bu[:, ST_PER_UG:]

    def group(r, carry):
        rows = pl.ds(pl.multiple_of(r * SUBLANES, SUBLANES), SUBLANES)
        for lc in range(S5_N // SCAN_LW):
            ls = slice(lc * SCAN_LW, (lc + 1) * SCAN_LW)
            xr = xr_scr[rows, ls]
            xi = xi_scr[rows, ls]
            for n, d in enumerate((1, 2, 4)):
                sr = pltpu.roll(xr, d, 0)
                si = pltpu.roll(xi, d, 0)
                are = tab_ref[2 * n, :, ls]
                aim = tab_ref[2 * n + 1, :, ls]
                xr, xi = xr + (are * sr - aim * si), xi + (are * si + aim * sr)
            cr = car_scr[0:1, ls]
            ci = car_scr[1:2, ls]
            are = tab_ref[6, :, ls]
            aim = tab_ref[7, :, ls]
            xr, xi = xr + (are * cr - aim * ci), xi + (are * ci + aim * cr)
            xr_scr[rows, ls] = xr
            xi_scr[rows, ls] = xi
            car_scr[0:1, ls] = xr[SUBLANES - 1:SUBLANES]
            car_scr[1:2, ls] = xi[SUBLANES - 1:SUBLANES]
        return carry

    lax.fori_loop(0, ts // SUBLANES, group, 0)

    ys = []
    for j in range(U_GROUPS):
        ss = slice(j * ST_PER_UG, (j + 1) * ST_PER_UG)
        yj = jnp.dot(xr_scr[:, ss].astype(bf16), cw_ref[j, :ST_PER_UG, :], preferred_element_type=f32)
        yj = yj + jnp.dot(xi_scr[:, ss].astype(bf16), cw_ref[j, ST_PER_UG:, :], preferred_element_type=f32)
        ys.append(yj)
    y = _gelu(jnp.concatenate(ys, axis=1) + d_ref[...] * u)
    z = jnp.dot(y.astype(bf16), glu_ref[...], preferred_element_type=f32)
    y_ref[...] = (y * _sigmoid(z)).astype(y_ref.dtype)

    @pl.when(t == pl.num_programs(1) - 1)
    def _():
        xr_out[...] = car_scr[0:1, :]
        xi_out[...] = car_scr[1:2, :]


def _s5(u, bw, tabs, cw, d_skip, w_glu, bsz, seq, ts):
    nt = seq // ts
    row = pl.BlockSpec((ts, S5_WIDTH), lambda b, j: (b * nt + j, 0))
    st = pl.BlockSpec((None, 1, S5_N), lambda b, j: (b, 0, 0))
    y, xr, xi = pl.pallas_call(
        functools.partial(_s5_kernel, ts=ts),
        out_shape=[
            jax.ShapeDtypeStruct((bsz * seq, S5_WIDTH), bf16),
            jax.ShapeDtypeStruct((bsz, 1, S5_N), f32),
            jax.ShapeDtypeStruct((bsz, 1, S5_N), f32),
        ],
        grid=(bsz, nt),
        in_specs=[
            row,
            _const_spec((U_GROUPS, LANES, 2 * ST_PER_UG), (0, 0, 0)),
            _const_spec((8, SUBLANES, S5_N), (0, 0, 0)),
            _const_spec((U_GROUPS, 2 * ST_PER_UG, LANES), (0, 0, 0)),
            _const_spec((1, S5_WIDTH), (0, 0)),
            _const_spec((S5_WIDTH, S5_WIDTH), (0, 0)),
        ],
        out_specs=[row, st, st],
        scratch_shapes=[pltpu.VMEM((ts, S5_N), f32), pltpu.VMEM((ts, S5_N), f32), pltpu.VMEM((SUBLANES, S5_N), f32)],
        compiler_params=_cparams(2),
        name="s5",
    )(u, bw, tabs, cw, d_skip, w_glu)
    return y, xr.reshape(bsz, S5_GROUPS, S5_STATE), xi.reshape(bsz, S5_GROUPS, S5_STATE)


FF_CHUNK = 256
FFN_HALO = SUBLANES


def _ffn_up_kernel(x_ref, xh_ref, sc_ref, sh_ref, g_ref, w_ref, cw_ref, cb_ref, act_ref, cs_ref, *, tiles_per_seq):
    i = pl.program_id(0)
    tm = x_ref.shape[0]
    g = g_ref[...]
    sc = sc_ref[...]
    sh = sh_ref[...]
    h = _norm_mod(x_ref[...], g, sc, sh).astype(bf16)
    hh = _norm_mod(xh_ref[...], g, sc, sh).astype(bf16)
    keep = (i % tiles_per_seq != 0).astype(f32)
    for n in range(D_FF // FF_CHUNK):
        mixed = []
        for half in range(2):
            cs = slice(half * D_FF + n * FF_CHUNK, half * D_FF + (n + 1) * FF_CHUNK)
            w = w_ref[:, cs]
            up = jnp.dot(h, w, preferred_element_type=f32)
            halo = jnp.dot(hh, w, preferred_element_type=f32) * keep
            ext = jnp.concatenate([halo, up], axis=0)
            r1 = pltpu.roll(ext, 1, 0)[FFN_HALO:]
            r2 = pltpu.roll(ext, 2, 0)[FFN_HALO:]
            mixed.append(cw_ref[0:1, cs] * r2 + cw_ref[1:2, cs] * r1 + cw_ref[2:3, cs] * up + cb_ref[:, cs])
            cs_ref[:, cs] = up[tm - (CONV_W - 1):, :]
        act_ref[:, n * FF_CHUNK:(n + 1) * FF_CHUNK] = (mixed[0] * _gelu(mixed[1])).astype(act_ref.dtype)


def _ffn_up(tr, x, li, g_pre, w_up, conv_w, conv_b, bsz):
    t = x.shape[0]
    tm = tr.tm
    tiles = tr.seq // tm
    halo_blocks = tm // FFN_HALO
    return pl.pallas_call(
        functools.partial(_ffn_up_kernel, tiles_per_seq=tiles),
        out_shape=[
            jax.ShapeDtypeStruct((t, D_FF), bf16),
            jax.ShapeDtypeStruct((bsz, CONV_W - 1, 2 * D_FF), f32),
        ],
        grid=(t // tm,),
        in_specs=[
            tr.row_spec(D_MODEL),
            pl.BlockSpec((FFN_HALO, D_MODEL), lambda i: (jnp.maximum(i * halo_blocks - 1, 0), 0)),
            tr.mod_spec(li, 4),
            tr.mod_spec(li, 3),
            _const_spec((None, 1, D_MODEL), (li, 0, 0)),
            _const_spec((None, D_MODEL, 2 * D_FF), (li, 0, 0)),
            _const_spec((None, CONV_W, 2 * D_FF), (li, 0, 0)),
            _const_spec((None, 1, 2 * D_FF), (li, 0, 0)),
        ],
        out_specs=[
            tr.row_spec(D_FF),
            pl.BlockSpec((None, CONV_W - 1, 2 * D_FF), lambda i: (i // tiles, 0, 0)),
        ],
        compiler_params=_cparams(1),
        name="ffn_up",
    )(x, x, tr.mod, tr.mod, g_pre, w_up, conv_w, conv_b)


def _t5_bucket(dist):
    n = jnp.maximum(dist, 0)
    max_exact = N_BUCKETS // 2
    large = max_exact + (jnp.log(jnp.maximum(n, 1).astype(f32) / max_exact)
                         / math.log(MAX_DISTANCE / max_exact) * (N_BUCKETS - max_exact)).astype(jnp.int32)
    large = jnp.minimum(large, N_BUCKETS - 1)
    return jnp.where(n < max_exact, n, large)


def _bias_by_distance(rel_bias, n):
    return rel_bias[_t5_bucket(jnp.arange(n))]


ATT_T = 512
SUFFIX_W = 256
SB_HEAD_GROUP = 4


def _diff_tile(q_ref, k_ref, v_ref, bias_ref, m_scr, l_scr, acc_scr, diagonal):
    t = q_ref.shape[0]
    if diagonal:
        causal = lax.broadcasted_iota(jnp.int32, (t, t), 0) >= lax.broadcasted_iota(jnp.int32, (t, t), 1)
    streams = range(2 * H_DIFF)
    scores = []
    for n in streams:
        sl = slice(n * DH_DIFF, (n + 1) * DH_DIFF)
        s = _dot_nt(q_ref[:, sl], k_ref[:, sl]) + bias_ref[n // 2]
        scores.append(jnp.where(causal, s, NEG_BIG) if diagonal else s)
    probs, alphas = [], []
    for n in streams:
        m_old = m_scr[n]
        m_new = jnp.maximum(m_old, jnp.max(scores[n], axis=-1, keepdims=True))
        p = jnp.exp(scores[n] - m_new)
        alpha = jnp.exp(m_old - m_new)
        l_scr[n] = alpha * l_scr[n] + jnp.sum(p, axis=-1, keepdims=True)
        m_scr[n] = m_new
        probs.append(p.astype(bf16))
        alphas.append(alpha)
    for n in streams:
        vh = v_ref[:, (n // 2) * 2 * DH_DIFF:(n // 2 + 1) * 2 * DH_DIFF]
        acc_scr[n] = alphas[n] * acc_scr[n] + jnp.dot(probs[n], vh, preferred_element_type=f32)


def _diff_attn_kernel(lam_ref, q_ref, k_ref, v_ref, bias_ref, sg_ref, o_ref, m_scr, l_scr, acc_scr, *, out_scale):
    qi = pl.program_id(1)
    kj = pl.program_id(2)

    @pl.when(kj == 0)
    def _():
        m_scr[...] = jnp.full_like(m_scr, NEG_BIG)
        l_scr[...] = jnp.zeros_like(l_scr)
        acc_scr[...] = jnp.zeros_like(acc_scr)

    @pl.when(kj < qi)
    def _():
        _diff_tile(q_ref, k_ref, v_ref, bias_ref, m_scr, l_scr, acc_scr, False)

    @pl.when(kj == qi)
    def _():
        _diff_tile(q_ref, k_ref, v_ref, bias_ref, m_scr, l_scr, acc_scr, True)
        lam = lam_ref[0]
        for hh in range(H_DIFF):
            o = acc_scr[2 * hh] / l_scr[2 * hh] - lam * (acc_scr[2 * hh + 1] / l_scr[2 * hh + 1])
            ms = jnp.mean(o * o, axis=-1, keepdims=True)
            o = o * lax.rsqrt(ms + EPS) * sg_ref[...] * out_scale
            o_ref[:, hh * 2 * DH_DIFF:(hh + 1) * 2 * DH_DIFF] = o.astype(o_ref.dtype)


def _diff_attn(q, k, v, bias_tiles, lam, subln_g, lam_init, bsz, seq):
    t = ATT_T
    nq = seq // t
    qspec = pl.BlockSpec((t, MIX_W), lambda b, i, j: (b * nq + i, 0))
    kspec = pl.BlockSpec((t, MIX_W), lambda b, i, j: (b * nq + jnp.minimum(j, i), 0))
    return pl.pallas_call(
        functools.partial(_diff_attn_kernel, out_scale=1.0 - lam_init),
        out_shape=jax.ShapeDtypeStruct((bsz * seq, MIX_W), bf16),
        grid=(bsz, nq, nq),
        in_specs=[
            pl.BlockSpec(memory_space=pltpu.SMEM),
            qspec,
            kspec,
            kspec,
            pl.BlockSpec((H_DIFF, None, t, t), lambda b, i, j: (0, jnp.maximum(i - j, 0), 0, 0)),
            _const_spec((1, 2 * DH_DIFF), (0, 0)),
        ],
        out_specs=qspec,
        scratch_shapes=[
            pltpu.VMEM((2 * H_DIFF, t, 1), f32),
            pltpu.VMEM((2 * H_DIFF, t, 1), f32),
            pltpu.VMEM((2 * H_DIFF, t, 2 * DH_DIFF), f32),
        ],
        compiler_params=_cparams(3),
        name="diff_attn",
    )(lam, q, k, v, bias_tiles, subln_g)


def _log_sigmoid_pair(z):
    ls = jnp.minimum(z, 0.0) - jnp.log(1.0 + jnp.exp(-jnp.abs(z)))
    return ls, ls - z


def _suffix_sum(lk, upper):
    hi, lo = _split(lk)
    return jnp.dot(hi, upper, preferred_element_type=f32) + jnp.dot(lo, upper, preferred_element_type=f32)


def _upper_ones(t):
    j = lax.broadcasted_iota(jnp.int32, (t, t), 0)
    k = lax.broadcasted_iota(jnp.int32, (t, t), 1)
    return jnp.where(j > k, 1.0, 0.0).astype(bf16)


def _sb_attn_kernel(q_ref, k_ref, v_ref, o_ref, r_scr, acc_scr):
    qi = pl.program_id(1)
    st = pl.program_id(2)
    t = q_ref.shape[0]

    @pl.when(st == 0)
    def _():
        r_scr[...] = jnp.zeros_like(r_scr)
        acc_scr[...] = jnp.zeros_like(acc_scr)

    def tile(diagonal):
        if diagonal:
            strict = lax.broadcasted_iota(jnp.int32, (t, t), 1) < lax.broadcasted_iota(jnp.int32, (t, t), 0)
        upper = _upper_ones(SUFFIX_W)
        nblk = t // SUFFIX_W
        for g0 in range(0, H_SB, SB_HEAD_GROUP):
            heads = range(g0, g0 + SB_HEAD_GROUP)
            pairs = {}
            for hh in heads:
                sl = slice(hh * DH_SB, (hh + 1) * DH_SB)
                ls, lk = _log_sigmoid_pair(_dot_nt(q_ref[:, sl], k_ref[:, sl]))
                pairs[hh] = (ls, jnp.where(strict, lk, 0.0) if diagonal else lk)
            sufs = {hh: [_suffix_sum(pairs[hh][1][:, c * SUFFIX_W:(c + 1) * SUFFIX_W], upper) for c in range(nblk)]
                    for hh in heads}
            ws = {}
            for hh in heads:
                ls, lk = pairs[hh]
                carry = r_scr[hh]
                afters = [None] * nblk
                for c in reversed(range(nblk)):
                    afters[c] = sufs[hh][c] + carry
                    carry = carry + jnp.sum(lk[:, c * SUFFIX_W:(c + 1) * SUFFIX_W], axis=-1, keepdims=True)
                r_scr[hh] = carry
                w = jnp.exp(ls + jnp.concatenate(afters, axis=1))
                ws[hh] = (jnp.where(strict, w, 0.0) if diagonal else w).astype(bf16)
            for hh in heads:
                sl = slice(hh * DH_SB, (hh + 1) * DH_SB)
                acc_scr[hh] = acc_scr[hh] + jnp.dot(ws[hh], v_ref[:, sl], preferred_element_type=f32)

    @pl.when(st == 0)
    def _():
        tile(True)

    @pl.when(jnp.logical_and(st > 0, st <= qi))
    def _():
        tile(False)

    @pl.when(st == qi)
    def _():
        for hh in range(H_SB):
            o_ref[:, hh * DH_SB:(hh + 1) * DH_SB] = acc_scr[hh].astype(o_ref.dtype)


def _sb_attn(q, k, v, bsz, seq):
    t = ATT_T
    nq = seq // t
    qspec = pl.BlockSpec((t, MIX_W), lambda b, i, s: (b * nq + i, 0))
    kspec = pl.BlockSpec((t, MIX_W), lambda b, i, s: (b * nq + jnp.maximum(i - s, 0), 0))
    return pl.pallas_call(
        _sb_attn_kernel,
        out_shape=jax.ShapeDtypeStruct((bsz * seq, MIX_W), bf16),
        grid=(bsz, nq, nq),
        in_specs=[qspec, kspec, kspec],
        out_specs=qspec,
        scratch_shapes=[pltpu.VMEM((H_SB, t, 1), f32), pltpu.VMEM((H_SB, t, DH_SB), f32)],
        compiler_params=_cparams(3),
        name="sb_attn",
    )(q, k, v)


DEC_BB = 8


def _even_dec_kernel(q_ref, k_ref, v_ref, g_ref, u_ref, s0_ref, x0r_ref, x0i_ref, gn_ref, bw_ref, are_ref, aim_ref,
                     cw_ref, d_ref, glu_ref, o_ref, y_ref, s_ref, xr_ref, xi_ref, *, decay):
    pad = jnp.zeros((DK_RET - DEC_BB, DK_RET), f32)
    for hh in range(H_RET):
        sl = slice(hh * DK_RET, (hh + 1) * DK_RET)
        qh = q_ref[:, sl]
        kh = k_ref[:, sl]
        vh = v_ref[:, sl]
        q_t = jnp.concatenate([qh, pad], axis=0).T
        k_t = jnp.concatenate([kh, pad], axis=0).T
        qk = jnp.sum(qh * kh, axis=-1, keepdims=True)
        rows = []
        for b in range(DEC_BB):
            s0 = s0_ref[b, hh]
            qs = jnp.sum(q_t[:, b:b + 1] * s0, axis=0, keepdims=True)
            rows.append(qk[b:b + 1] * vh[b:b + 1] + qs * decay[hh])
            s_ref[b, hh] = s0 * decay[hh] + k_t[:, b:b + 1] * vh[b:b + 1]
        o = jnp.concatenate(rows, axis=0)
        mu = jnp.mean(o, axis=-1, keepdims=True)
        dlt = o - mu
        var = jnp.mean(dlt * dlt, axis=-1, keepdims=True)
        on = dlt * lax.rsqrt(var + EPS) * gn_ref[hh]
        o_ref[:, sl] = on * _silu(g_ref[:, sl])

    u = u_ref[...]
    bus = [_dot(_lhs(u[:, j * LANES:(j + 1) * LANES], 3), bw_ref[j], 3) for j in range(U_GROUPS)]
    bu_re = jnp.concatenate([b[:, :ST_PER_UG] for b in bus], axis=1)
    bu_im = jnp.concatenate([b[:, ST_PER_UG:] for b in bus], axis=1)
    are = are_ref[...]
    aim = aim_ref[...]
    x0r = x0r_ref[...]
    x0i = x0i_ref[...]
    xr = (are * x0r - aim * x0i) + bu_re
    xi = (are * x0i + aim * x0r) + bu_im
    xr_ref[...] = xr
    xi_ref[...] = xi
    ys = []
    for j in range(U_GROUPS):
        ss = slice(j * ST_PER_UG, (j + 1) * ST_PER_UG)
        ys.append(_dot(_lhs(xr[:, ss], 3), cw_ref[j, :ST_PER_UG, :], 3) + _dot(_lhs(xi[:, ss], 3), cw_ref[j, ST_PER_UG:, :], 3))
    y = _gelu(jnp.concatenate(ys, axis=1) + d_ref[...] * u)
    z = _dot(_lhs(y, 3), glu_ref[...], 3)
    y_ref[...] = y * _sigmoid(z)


def _even_dec(q, k, v, g, u, state_ret, s5_re, s5_im, ie, gn_g, bw, a_re, a_im, cw, d_skip, w_glu):
    nb = q.shape[0]
    row = pl.BlockSpec((DEC_BB, MIX_W), lambda i: (i, 0))
    st_spec = pl.BlockSpec((None, DEC_BB, H_RET, DK_RET, DV_RET), lambda i: (ie, i, 0, 0, 0))
    x_spec = pl.BlockSpec((None, DEC_BB, S5_N), lambda i: (ie, i, 0))
    decay = tuple(float(math.exp(math.log1p(-(2.0 ** (-5.0 - h))))) for h in range(H_RET))
    return pl.pallas_call(
        functools.partial(_even_dec_kernel, decay=decay),
        out_shape=[
            jax.ShapeDtypeStruct((nb, MIX_W), f32),
            jax.ShapeDtypeStruct((nb, S5_WIDTH), f32),
            jax.ShapeDtypeStruct((nb, H_RET, DK_RET, DV_RET), f32),
            jax.ShapeDtypeStruct((nb, S5_N), f32),
            jax.ShapeDtypeStruct((nb, S5_N), f32),
        ],
        grid=(nb // DEC_BB,),
        in_specs=[
            row, row, row, row, row, st_spec, x_spec, x_spec,
            _const_spec((H_RET, 1, DV_RET), (0, 0, 0)),
            _const_spec((U_GROUPS, LANES, 2 * ST_PER_UG), (0, 0, 0)),
            _const_spec((1, S5_N), (0, 0)),
            _const_spec((1, S5_N), (0, 0)),
            _const_spec((U_GROUPS, 2 * ST_PER_UG, LANES), (0, 0, 0)),
            _const_spec((1, S5_WIDTH), (0, 0)),
            _const_spec((S5_WIDTH, S5_WIDTH), (0, 0)),
        ],
        out_specs=[
            row, row,
            pl.BlockSpec((DEC_BB, H_RET, DK_RET, DV_RET), lambda i: (i, 0, 0, 0)),
            pl.BlockSpec((DEC_BB, S5_N), lambda i: (i, 0)),
            pl.BlockSpec((DEC_BB, S5_N), lambda i: (i, 0)),
        ],
        compiler_params=_cparams(1),
        name="even_dec",
    )(q, k, v, g, u, state_ret, s5_re, s5_im, gn_g.reshape(H_RET, 1, DV_RET), bw, a_re, a_im, cw, d_skip, w_glu)


def _ffn_up_dec_kernel(x_ref, sc_ref, sh_ref, g_ref, wv_ref, wg_ref, b0v_ref, b0g_ref, b1v_ref, b1g_ref,
                       cwv_ref, cwg_ref, cbv_ref, cbg_ref, act_ref, upv_ref, upg_ref):
    h = _lhs(_norm_mod(x_ref[...], g_ref[...], sc_ref[...], sh_ref[...]), 3)
    upv = _dot(h, wv_ref[...], 3)
    upg = _dot(h, wg_ref[...], 3)
    val = cwv_ref[0:1, :] * b0v_ref[...] + cwv_ref[1:2, :] * b1v_ref[...] + cwv_ref[2:3, :] * upv + cbv_ref[...]
    gate = cwg_ref[0:1, :] * b0g_ref[...] + cwg_ref[1:2, :] * b1g_ref[...] + cwg_ref[2:3, :] * upg + cbg_ref[...]
    act_ref[...] = val * _gelu(gate)
    upv_ref[...] = upv
    upg_ref[...] = upg


def _ffn_up_dec(x, mod_s, li, g_pre, w_up, conv_w, conv_b, state_conv2d):
    nb = x.shape[0]
    nch = D_FF // FF_CHUNK
    full = _const_spec((nb, D_MODEL), (0, 0))

    def mod(k):
        return pl.BlockSpec((None, nb, D_MODEL), lambda n: (li, 0, k))

    def cols(rows, off):
        return pl.BlockSpec((None, rows, FF_CHUNK), lambda n: (li, 0, n + off))

    out = pl.BlockSpec((nb, FF_CHUNK), lambda n: (0, n))
    return pl.pallas_call(
        _ffn_up_dec_kernel,
        out_shape=[jax.ShapeDtypeStruct((nb, D_FF), f32)] * 3,
        grid=(nch,),
        in_specs=[
            full, mod(4), mod(3), _const_spec((None, 1, D_MODEL), (li, 0, 0)),
            cols(D_MODEL, 0), cols(D_MODEL, nch),
            cols(nb, 0), cols(nb, nch), cols(nb, 2 * nch), cols(nb, 3 * nch),
            cols(CONV_W, 0), cols(CONV_W, nch), cols(1, 0), cols(1, nch),
        ],
        out_specs=[out, out, out],
        compiler_params=_cparams(1),
        name="ffn_up_dec",
    )(x, mod_s, mod_s, g_pre, w_up, w_up, state_conv2d, state_conv2d, state_conv2d, state_conv2d,
      conv_w, conv_w, conv_b, conv_b)


DEC_PP = 8


def _attn_dec_kernel(pt_ref, lam_ref, qd_ref, qs_ref, kn_ref, vn_ref, bias_ref, bias0_ref, sg_ref, *rest, n_pages, out_scale):
    del pt_ref
    pages = rest[:4 * DEC_PP]
    o_ref, md, ld, accd, rs, accs = rest[4 * DEC_PP:]
    j = pl.program_id(1)
    rows = H_SB
    hd = 2 * DH_DIFF
    head_of_lane = lax.broadcasted_iota(jnp.int32, (rows, MIX_W), 1) // DH_SB
    q_sb = jnp.where(head_of_lane == lax.broadcasted_iota(jnp.int32, (rows, MIX_W), 0), qs_ref[...], 0.0)

    def per_row_head(ref):
        return jnp.concatenate([ref[:, (r // 2) * hd:(r // 2 + 1) * hd] for r in range(rows)], axis=0)

    half_of_lane = lax.broadcasted_iota(jnp.int32, (rows, hd), 1) // DH_DIFF
    row_id = lax.broadcasted_iota(jnp.int32, (rows, hd), 0)
    q_diff = jnp.where(half_of_lane == row_id % 2, per_row_head(qd_ref), 0.0)

    @pl.when(j == 0)
    def _():
        md[...] = jnp.sum(q_diff * per_row_head(kn_ref), axis=-1, keepdims=True) + bias0_ref[...]
        ld[...] = jnp.ones_like(ld)
        accd[...] = per_row_head(vn_ref)
        rs[...] = jnp.zeros_like(rs)
        accs[...] = jnp.zeros_like(accs)

    qd_b = q_diff.astype(bf16)
    qs_b = q_sb.astype(bf16)
    upper = _upper_ones(PAGE_SIZE)

    flat = PAGE_SIZE * H_DIFF
    own = lax.broadcasted_iota(jnp.int32, (rows, flat), 1) % H_DIFF == lax.broadcasted_iota(jnp.int32, (rows, flat), 0) // 2
    scores = []
    for i in range(DEC_PP):
        page = n_pages - 1 - (j * DEC_PP + i)
        s = _dot_nt(qd_b, pages[4 * i][...].astype(bf16)) + bias_ref[page]
        scores.append(jnp.where(own, s, NEG_BIG))
    m_old = md[...]
    m_new = m_old
    for s in scores:
        m_new = jnp.maximum(m_new, jnp.max(s, axis=-1, keepdims=True))
    alpha = jnp.exp(m_old - m_new)
    l_new = alpha * ld[...]
    acc_d = alpha * accd[...]
    for i, s in enumerate(scores):
        p = jnp.exp(s - m_new)
        l_new = l_new + jnp.sum(p, axis=-1, keepdims=True)
        acc_d = acc_d + jnp.dot(p.astype(bf16), pages[4 * i + 1][...].astype(bf16), preferred_element_type=f32)
    ld[...] = l_new
    md[...] = m_new
    accd[...] = acc_d

    zs = [jnp.dot(qs_b, pages[4 * i + 2][...].reshape(MIX_W, PAGE_SIZE).astype(bf16), preferred_element_type=f32)
          for i in range(DEC_PP)]
    pairs = [_log_sigmoid_pair(z) for z in zs]
    sufs = [_suffix_sum(lk, upper) for _, lk in pairs]
    carry = rs[...]
    ws = []
    for (ls, lk), suf in zip(pairs, sufs):
        ws.append(jnp.exp(ls + (suf + carry)).astype(bf16))
        carry = carry + jnp.sum(lk, axis=-1, keepdims=True)
    acc_s = accs[...]
    for i, w in enumerate(ws):
        acc_s = acc_s + _dot_nt(w, pages[4 * i + 3][...].reshape(MIX_W, PAGE_SIZE).astype(bf16))
    rs[...] = carry
    accs[...] = acc_s

    @pl.when(j == pl.num_programs(1) - 1)
    def _():
        lam = lam_ref[0]
        acc = accd[...] / ld[...]
        pieces = []
        for hh in range(H_DIFF):
            o = acc[2 * hh:2 * hh + 1] - lam * acc[2 * hh + 1:2 * hh + 2]
            ms = jnp.mean(o * o, axis=-1, keepdims=True)
            pieces.append(o * lax.rsqrt(ms + EPS) * sg_ref[...] * out_scale)
        a_sb = accs[...]
        for hh in range(H_SB):
            pieces.append(a_sb[hh:hh + 1, hh * DH_SB:(hh + 1) * DH_SB])
        o_ref[...] = jnp.concatenate(pieces, axis=1)


def _attn_dec(page_table, lam, qd, qs, k_new, v_new, bias_pages, bias0, subln_g, caches, io, lam_init):
    nb, n_pages = page_table.shape
    row3 = lambda a: a.reshape(nb, 1, a.shape[-1])
    rspec = pl.BlockSpec((None, 1, MIX_W), lambda b, j, pt: (b, 0, 0))

    def page_spec(i, cache):
        blk = (None, None) + cache.shape[2:]
        zeros = (0,) * (cache.ndim - 2)
        return pl.BlockSpec(blk, lambda b, j, pt: (io, pt[b, n_pages - 1 - (j * DEC_PP + i)]) + zeros)

    page_specs, page_args = [], []
    for i in range(DEC_PP):
        for cache in caches:
            page_specs.append(page_spec(i, cache))
            page_args.append(cache)
    grid_spec = pltpu.PrefetchScalarGridSpec(
        num_scalar_prefetch=1,
        grid=(nb, n_pages // DEC_PP),
        in_specs=[
            pl.BlockSpec(memory_space=pltpu.SMEM),
            rspec, rspec, rspec, rspec,
            pl.BlockSpec((n_pages, 1, PAGE_SIZE * H_DIFF), lambda b, j, pt: (0, 0, 0)),
            pl.BlockSpec((H_SB, 1), lambda b, j, pt: (0, 0)),
            pl.BlockSpec((1, 2 * DH_DIFF), lambda b, j, pt: (0, 0)),
        ] + page_specs,
        out_specs=pl.BlockSpec((None, 1, 2 * MIX_W), lambda b, j, pt: (b, 0, 0)),
        scratch_shapes=[
            pltpu.VMEM((H_SB, 1), f32), pltpu.VMEM((H_SB, 1), f32), pltpu.VMEM((H_SB, 2 * DH_DIFF), f32),
            pltpu.VMEM((H_SB, 1), f32), pltpu.VMEM((H_SB, MIX_W), f32),
        ],
    )
    out = pl.pallas_call(
        functools.partial(_attn_dec_kernel, n_pages=n_pages, out_scale=1.0 - lam_init),
        out_shape=jax.ShapeDtypeStruct((nb, 1, 2 * MIX_W), f32),
        grid_spec=grid_spec,
        compiler_params=_cparams(2),
        name="attn_dec",
    )(page_table, lam, row3(qd), row3(qs), row3(k_new), row3(v_new), bias_pages, bias0, subln_g, *page_args)
    return out.reshape(nb, 2 * MIX_W)


def _lam_init(li):
    return 0.8 - 0.6 * math.exp(-0.3 * li)


def _diff_lambda(lam_vecs, li):
    lv = lam_vecs.astype(f32)
    lam = jnp.exp(jnp.dot(lv[0], lv[1])) - jnp.exp(jnp.dot(lv[2], lv[3])) + _lam_init(li)
    return lam.reshape(1)


def _rope_tables(pos):
    half = DK_RET // 2
    inv = jnp.power(ROPE_BASE, -jnp.arange(half, dtype=f32) / half)
    ang = pos.astype(f32)[:, None] * inv[None, :]
    cos, sin = jnp.cos(ang), jnp.sin(ang)
    return jnp.concatenate([cos, cos], axis=-1), jnp.concatenate([-sin, sin], axis=-1)


def _prompt_bias_tiles(bias_d, seq):
    t = ATT_T
    nd = seq // t
    m = jnp.arange(2 * t)
    r_minus_c = jnp.where(m < t, -m, 2 * t - m)
    dist = jnp.arange(nd)[:, None] * t + r_minus_c[None, :]
    gen = jnp.transpose(bias_d[jnp.clip(dist, 0, seq - 1)], (2, 0, 1))
    rows = jnp.tile(gen, (1, 1, t))[:, :, :t * (2 * t - 1)].reshape(gen.shape[0], nd, t, 2 * t - 1)
    return rows[:, :, :, :t]


def _prompt_trunk(x_prompt, mod_p, prm):
    bsz, seq, _ = x_prompt.shape
    tm = min(512, seq)
    tr = _Trunk(mod_p, seq, tm, 1)
    x = x_prompt.reshape(bsz * seq, D_MODEL)
    cos2, sin2 = _rope_tables(jnp.arange(seq))
    bias_tiles = _prompt_bias_tiles(_bias_by_distance(prm["rel_bias"], seq), seq)
    new = {k: [] for k in ("ret", "s5r", "s5i", "kd", "vd", "ks", "vs", "conv")}
    for li in range(DEPTH):
        if li % 2 == 0:
            ie = li // 2
            s5p = prm["s5"][ie]
            q, k, v, g, u = _proj_even(tr, x, li, prm["g_pre_mix"], prm["w_in_even_b"], cos2, sin2, (bf16, bf16, bf16, f32, f32))
            o, ret_s = _retention(q, k, v, g, prm["ret_gn_g"][ie], bsz, seq)
            y, xr, xi = _s5(u, s5p["bw"].astype(bf16), s5p["tabs"], s5p["cw"].astype(bf16), s5p["d"], s5p["glu_b"], bsz, seq,
                            min(256, seq))
            new["ret"].append(ret_s)
            new["s5r"].append(xr)
            new["s5i"].append(xi)
            x = _outproj(tr, [o, y], prm["w_out_even_b"], ie, x, li, 2, prm["g_post_mix"])
        else:
            io = li // 2
            dq, dk, dv, sq, sk, sv, dk_b, dv_b, sk_b, sv_b = _proj_odd(tr, x, li, prm["g_pre_mix"], prm["w_in_odd_b"], True)
            lam = _diff_lambda(prm["diff_lam"][io], li)
            od = _diff_attn(dq, dk_b, dv_b, bias_tiles, lam, prm["diff_subln_g"][io].reshape(1, -1), _lam_init(li), bsz, seq)
            osb = _sb_attn(sq, sk_b, sv_b, bsz, seq)
            new["kd"].append(dk.reshape(bsz, seq, H_DIFF, 2 * DH_DIFF))
            new["vd"].append(dv.reshape(bsz, seq, H_DIFF, 2 * DH_DIFF))
            new["ks"].append(sk.reshape(bsz, seq, H_SB, DH_SB))
            new["vs"].append(sv.reshape(bsz, seq, H_SB, DH_SB))
            x = _outproj(tr, [od, osb], prm["w_out_odd_b"], io, x, li, 2, prm["g_post_mix"])
        act, cs = _ffn_up(tr, x, li, prm["g_pre_ffn"], prm["w_up_b"], prm["conv_w"], prm["conv_b"], bsz)
        new["conv"].append(cs)
        x = _outproj(tr, [act], prm["w_down_b"], li, x, li, 5, prm["g_post_ffn"])
    st = lambda t: jnp.stack(t, axis=0)
    return x.reshape(bsz, seq, D_MODEL), tuple(st(new[k]) for k in ("ret", "s5r", "s5i", "kd", "vd", "ks", "vs", "conv"))


def _sample_trunk(x_sample, mod_s, prm, past):
    nb = x_sample.shape[0]
    tr = _Trunk(mod_s, 1, nb, 3)
    x = x_sample.reshape(nb, D_MODEL)
    page_table = past["page_table"]
    n_pages = page_table.shape[1]
    q_off = n_pages * PAGE_SIZE
    cos2, sin2 = _rope_tables(jnp.full((1,), q_off))
    bias_d = _bias_by_distance(prm["rel_bias"], q_off + 1)
    key_pos = jnp.arange(q_off).reshape(n_pages, PAGE_SIZE)
    bias_pages = bias_d[q_off - key_pos].reshape(n_pages, 1, PAGE_SIZE * H_DIFF)
    bias0 = jnp.repeat(bias_d[0], 2).reshape(2 * H_DIFF, 1)
    s5_re = past["s5_re"].reshape(past["s5_re"].shape[0], nb, S5_N)
    s5_im = past["s5_im"].reshape(past["s5_im"].shape[0], nb, S5_N)
    conv2d = past["conv"].reshape(DEPTH, nb, (CONV_W - 1) * 2 * D_FF)
    caches = tuple(past[k].reshape(past[k].shape[0], past[k].shape[1], PAGE_SIZE * H_DIFF, 2 * DH_DIFF)
                   for k in ("diff_k", "diff_v"))
    caches += tuple(jnp.transpose(past[k], (0, 1, 3, 4, 2)) for k in ("sb_k", "sb_v"))
    new = {k: [] for k in ("ret", "s5r", "s5i", "kd", "vd", "ks", "vs", "conv")}
    for li in range(DEPTH):
        if li % 2 == 0:
            ie = li // 2
            s5p = prm["s5"][ie]
            q, k, v, g, u = _proj_even(tr, x, li, prm["g_pre_mix"], prm["w_in_even"], cos2, sin2, (f32,) * 5)
            o, y, ret_s, xr, xi = _even_dec(q, k, v, g, u, past["ret"], s5_re, s5_im, ie, prm["ret_gn_g"][ie],
                                            s5p["bw"], s5p["a_re"], s5p["a_im"], s5p["cw"], s5p["d"], s5p["glu"])
            new["ret"].append(ret_s)
            new["s5r"].append(xr.reshape(nb, S5_GROUPS, S5_STATE))
            new["s5i"].append(xi.reshape(nb, S5_GROUPS, S5_STATE))
            x = _outproj(tr, [o, y], prm["w_out_even"], ie, x, li, 2, prm["g_post_mix"])
        else:
            io = li // 2
            dq, dk, dv, sq, sk, sv = _proj_odd(tr, x, li, prm["g_pre_mix"], prm["w_in_odd"], False)
            lam = _diff_lambda(prm["diff_lam"][io], li)
            mix = _attn_dec(page_table, lam, dq, sq, dk, dv, bias_pages, bias0, prm["diff_subln_g"][io].reshape(1, -1),
                            caches, io, _lam_init(li))
            new["kd"].append(dk.reshape(nb, 1, H_DIFF, 2 * DH_DIFF))
            new["vd"].append(dv.reshape(nb, 1, H_DIFF, 2 * DH_DIFF))
            new["ks"].append(sk.reshape(nb, 1, H_SB, DH_SB))
            new["vs"].append(sv.reshape(nb, 1, H_SB, DH_SB))
            x = _outproj(tr, [mix[:, :MIX_W], mix[:, MIX_W:]], prm["w_out_odd"], io, x, li, 2, prm["g_post_mix"])
        act, upv, upg = _ffn_up_dec(x, mod_s, li, prm["g_pre_ffn"], prm["w_up"], prm["conv_w"], prm["conv_b"], conv2d)
        up = jnp.concatenate([upv, upg], axis=-1)
        new["conv"].append(jnp.stack([past["conv"][li][:, CONV_W - 2], up], axis=1))
        x = _outproj(tr, [act], prm["w_down"], li, x, li, 5, prm["g_post_ffn"])
    st = lambda t: jnp.stack(t, axis=0)
    return x.reshape(nb, 1, D_MODEL), tuple(st(new[k]) for k in ("ret", "s5r", "s5i", "kd", "vd", "ks", "vs", "conv"))


def _prepare_params(g_pre_mix, g_post_mix, g_pre_ffn, g_post_ffn, w_in_even, w_out_even, ret_gn_g, s5_lam_re, s5_lam_im,
                    s5_log_step, s5_b_re, s5_b_im, s5_c_re, s5_c_im, s5_d, s5_w_glu, w_in_odd, w_out_odd, diff_lam,
                    diff_subln_g, rel_bias, w_up, conv_w, conv_b, w_down):
    g3 = lambda g: g.reshape(g.shape[0], 1, g.shape[1])
    prm = dict(
        g_pre_mix=g3(g_pre_mix), g_post_mix=g3(g_post_mix), g_pre_ffn=g3(g_pre_ffn), g_post_ffn=g3(g_post_ffn),
        w_in_even=w_in_even, w_out_even=w_out_even, w_in_odd=w_in_odd, w_out_odd=w_out_odd, w_up=w_up, w_down=w_down,
        w_in_even_b=w_in_even.astype(bf16), w_out_even_b=w_out_even.astype(bf16), w_in_odd_b=w_in_odd.astype(bf16),
        w_out_odd_b=w_out_odd.astype(bf16), w_up_b=w_up.astype(bf16), w_down_b=w_down.astype(bf16),
        ret_gn_g=ret_gn_g, diff_lam=diff_lam, diff_subln_g=diff_subln_g, rel_bias=rel_bias,
        conv_w=conv_w, conv_b=g3(conv_b),
    )
    s5 = []
    for ie in range(s5_lam_re.shape[0]):
        a_re, a_im, bw, cw = _s5_params(s5_lam_re[ie], s5_lam_im[ie], s5_log_step[ie], s5_b_re[ie], s5_b_im[ie],
                                        s5_c_re[ie], s5_c_im[ie])
        s5.append(dict(a_re=a_re, a_im=a_im, bw=bw, cw=cw, tabs=_s5_scan_tables(a_re, a_im),
                       d=s5_d[ie].reshape(1, S5_WIDTH), glu=s5_w_glu[ie], glu_b=s5_w_glu[ie].astype(bf16)))
    prm["s5"] = s5
    return prm


def kernel(x_prompt, x_sample, c_prompt, c_sample, state_ret, state_s5_re, state_s5_im, cache_diff_k, cache_diff_v, cache_sb_k, cache_sb_v, state_conv, page_table, w_ada, b_ada, g_pre_mix, g_post_mix, g_pre_ffn, g_post_ffn, w_in_even, w_out_even, ret_gn_g, s5_lam_re, s5_lam_im, s5_log_step, s5_b_re, s5_b_im, s5_c_re, s5_c_im, s5_d, s5_w_glu, w_in_odd, w_out_odd, diff_lam, diff_subln_g, rel_bias, w_up, conv_w, conv_b, w_down):
    nb = x_sample.shape[0]
    bsz = x_prompt.shape[0]
    prm = _prepare_params(g_pre_mix, g_post_mix, g_pre_ffn, g_post_ffn, w_in_even, w_out_even, ret_gn_g, s5_lam_re,
                          s5_lam_im, s5_log_step, s5_b_re, s5_b_im, s5_c_re, s5_c_im, s5_d, s5_w_glu, w_in_odd, w_out_odd,
                          diff_lam, diff_subln_g, rel_bias, w_up, conv_w, conv_b, w_down)
    mod = _ada(jnp.concatenate([c_sample, c_prompt], axis=0), w_ada, b_ada)
    mod_s = mod[:, :nb]
    mod_p = mod[:, nb:].reshape(DEPTH, bsz, 1, 6 * D_MODEL)
    past = dict(ret=state_ret, s5_re=state_s5_re, s5_im=state_s5_im, diff_k=cache_diff_k, diff_v=cache_diff_v,
                sb_k=cache_sb_k, sb_v=cache_sb_v, conv=state_conv, page_table=page_table)
    y_prompt, sp = _prompt_trunk(x_prompt, mod_p, prm)
    y_sample, ss = _sample_trunk(x_sample, mod_s, prm, past)
    ret_p, s5r_p, s5i_p, kd_p, vd_p, ks_p, vs_p, conv_p = sp
    ret_s, s5r_s, s5i_s, kd_s, vd_s, ks_s, vs_s, conv_s = ss
    return (y_prompt, y_sample, ret_p, ret_s, s5r_p, s5r_s, s5i_p, s5i_s, kd_p, kd_s, vd_p, vd_s,
            ks_p, ks_s, vs_p, vs_s, conv_p, conv_s)
```

```python
import functools
import math

import jax
import jax.numpy as jnp
from jax import lax
from jax.experimental import pallas as pl
from jax.experimental.pallas import tpu as pltpu

f32 = jnp.float32
bf16 = jnp.bfloat16

D_MODEL = 1024
DEPTH = 4
PAST_LEN = 8192
PAGE_SIZE = 128
H_RET = 4
DK_RET = 128
DV_RET = 128
RET_CHUNK = 128
ROPE_BASE = 10000.0
S5_WIDTH = D_MODEL // 2
S5_GROUP = 16
S5_GROUPS = S5_WIDTH // S5_GROUP
S5_STATE = 64
S5_N = S5_GROUPS * S5_STATE
H_DIFF = 4
DH_DIFF = 64
H_SB = 8
DH_SB = 64
N_BUCKETS = 32
MAX_DISTANCE = 128
D_FF = 2816
CONV_W = 3
EPS = 1e-6
MIX_W = 512

LANES = 128
SUBLANES = 8
VMEM_LIMIT = 56 * 1024 * 1024
NEG_BIG = -1e30


def _cparams(n_axes):
    return pltpu.CompilerParams(dimension_semantics=("arbitrary",) * n_axes, vmem_limit_bytes=VMEM_LIMIT)


def _split(a):
    hi = a.astype(bf16)
    lo = (a - hi.astype(f32)).astype(bf16)
    return hi, lo


def _lhs(a, passes):
    if passes == 1:
        return (a.astype(bf16),)
    return _split(a.astype(f32))


def _dot(lhs, b, passes):
    if passes == 1:
        return jnp.dot(lhs[0], b.astype(bf16), preferred_element_type=f32)
    b_hi, b_lo = _split(b.astype(f32))
    return jnp.dot(lhs[0], b_hi, preferred_element_type=f32) + (
        jnp.dot(lhs[0], b_lo, preferred_element_type=f32) + jnp.dot(lhs[1], b_hi, preferred_element_type=f32)
    )


def _dot_nt(a, b):
    return lax.dot_general(a, b, (((1,), (1,)), ((), ())), preferred_element_type=f32)


def _norm_mod(x, g, sc, sh):
    ms = jnp.mean(x * x, axis=-1, keepdims=True)
    return (x * lax.rsqrt(ms + EPS) * g) * (1.0 + sc) + sh


def _gelu(x):
    return 0.5 * x * (1.0 + jnp.tanh(math.sqrt(2.0 / math.pi) * (x + 0.044715 * (x * x * x))))


def _sigmoid(x):
    return 1.0 / (1.0 + jnp.exp(-x))


def _silu(x):
    return x * _sigmoid(x)


def _ada_kernel(c_ref, w_ref, b_ref, o_ref):
    s = _silu(c_ref[...])
    o_ref[...] = _dot(_lhs(s, 3), w_ref[...], 3) + b_ref[...]


def _ada(c_all, w_ada, b_ada):
    n = c_all.shape[0]
    tn = 1024
    return pl.pallas_call(
        _ada_kernel,
        out_shape=jax.ShapeDtypeStruct((DEPTH, n, 6 * D_MODEL), f32),
        grid=(DEPTH, 6 * D_MODEL // tn),
        in_specs=[
            pl.BlockSpec((n, D_MODEL), lambda l, j: (0, 0)),
            pl.BlockSpec((None, D_MODEL, tn), lambda l, j: (l, 0, j)),
            pl.BlockSpec((None, 1, tn), lambda l, j: (l, 0, j)),
        ],
        out_specs=pl.BlockSpec((None, n, tn), lambda l, j: (l, 0, j)),
        compiler_params=_cparams(2),
        name="ada_mod",
    )(c_all, w_ada, b_ada.reshape(DEPTH, 1, 6 * D_MODEL))


class _Trunk:
    def __init__(self, mod, seq, tm, passes):
        self.mod = mod
        self.seq = seq
        self.tm = tm
        self.passes = passes
        self.per_row = seq == 1

    def mod_spec(self, li, k):
        if self.per_row:
            return pl.BlockSpec((None, self.tm, D_MODEL), lambda i: (li, i, k))
        tiles_per_seq = self.seq // self.tm
        return pl.BlockSpec((None, None, 1, D_MODEL), lambda i: (li, i // tiles_per_seq, 0, k))

    def row_spec(self, width):
        return pl.BlockSpec((self.tm, width), lambda i: (i, 0))


def _const_spec(shape, idx):
    return pl.BlockSpec(shape, lambda *_: idx)


def _rope(a, cos2, sin2):
    return a * cos2 + pltpu.roll(a, DK_RET // 2, 1) * sin2


def _proj_even_kernel(x_ref, sc_ref, sh_ref, g_ref, w_ref, cos_ref, sin_ref, q_ref, k_ref, v_ref, gg_ref, u_ref, *, passes):
    h = _lhs(_norm_mod(x_ref[...], g_ref[...], sc_ref[...], sh_ref[...]), passes)
    cos2 = cos_ref[...]
    sin2 = sin_ref[...]
    outs = (q_ref, k_ref, v_ref, gg_ref, u_ref)
    for n, ref in enumerate(outs):
        acc = _dot(h, w_ref[:, n * MIX_W:(n + 1) * MIX_W], passes)
        if n < 2:
            scale = 1.0 if n == 0 else DK_RET ** -0.5
            for hh in range(H_RET):
                sl = slice(hh * DK_RET, (hh + 1) * DK_RET)
                ref[:, sl] = (_rope(acc[:, sl], cos2, sin2) * scale).astype(ref.dtype)
        else:
            ref[...] = acc.astype(ref.dtype)


def _proj_even(tr, x, li, g_pre, w_in, cos2, sin2, out_dtypes):
    t = x.shape[0]
    tm = tr.tm
    n_in = w_in.shape[-1]
    if tr.per_row:
        cs_spec = pl.BlockSpec((1, DK_RET), lambda i: (0, 0))
    else:
        tiles = tr.seq // tm
        cs_spec = pl.BlockSpec((tm, DK_RET), lambda i: (i % tiles, 0))
    out_shapes = [jax.ShapeDtypeStruct((t, MIX_W), dt) for dt in out_dtypes]
    out_specs = [tr.row_spec(MIX_W) for _ in out_dtypes]
    if not tr.per_row:
        out_shapes[4] = jax.ShapeDtypeStruct((tr.seq, (t // tr.seq) * MIX_W), out_dtypes[4])
        out_specs[4] = pl.BlockSpec((tm, MIX_W), lambda i: (i % tiles, i // tiles))
    return pl.pallas_call(
        functools.partial(_proj_even_kernel, passes=tr.passes),
        out_shape=out_shapes,
        grid=(t // tm,),
        in_specs=[
            tr.row_spec(D_MODEL),
            tr.mod_spec(li, 1),
            tr.mod_spec(li, 0),
            _const_spec((None, 1, D_MODEL), (li, 0, 0)),
            _const_spec((None, D_MODEL, n_in), (li // 2, 0, 0)),
            cs_spec,
            cs_spec,
        ],
        out_specs=out_specs,
        compiler_params=_cparams(1),
        name="proj_even",
    )(x, tr.mod, tr.mod, g_pre, w_in, cos2, sin2)


def _proj_odd_kernel(x_ref, sc_ref, sh_ref, g_ref, w_ref, *out_refs, passes, with_copies):
    h = _lhs(_norm_mod(x_ref[...], g_ref[...], sc_ref[...], sh_ref[...]), passes)
    for n in range(6):
        acc = _dot(h, w_ref[:, n * MIX_W:(n + 1) * MIX_W], passes)
        if n in (0, 3):
            acc = acc * (DH_DIFF ** -0.5 if n == 0 else DH_SB ** -0.5)
        if with_copies and n in (1, 2):
            tm = acc.shape[0]
            hd = 2 * DH_DIFF
            for hh in range(H_DIFF):
                out_refs[n][pl.ds(hh, tm, stride=H_DIFF), :] = acc[:, hh * hd:(hh + 1) * hd]
        else:
            out_refs[n][...] = acc.astype(out_refs[n].dtype)
        if with_copies and n not in (0, 3):
            out_refs[6 + (n if n < 3 else n - 1) - 1][...] = acc.astype(bf16)


def _proj_odd(tr, x, li, g_pre, w_in, with_copies):
    t = x.shape[0]
    q_dt = bf16 if with_copies else f32
    dts = [q_dt, f32, f32, q_dt, f32, f32] + ([bf16] * 4 if with_copies else [])
    shapes = [(t, MIX_W)] * len(dts)
    specs = [tr.row_spec(MIX_W) for _ in dts]
    if with_copies:
        for n in (1, 2):
            shapes[n] = (t * H_DIFF, 2 * DH_DIFF)
            specs[n] = pl.BlockSpec((tr.tm * H_DIFF, 2 * DH_DIFF), lambda i: (i, 0))
    return pl.pallas_call(
        functools.partial(_proj_odd_kernel, passes=tr.passes, with_copies=with_copies),
        out_shape=[jax.ShapeDtypeStruct(s, dt) for s, dt in zip(shapes, dts)],
        grid=(t // tr.tm,),
        in_specs=[
            tr.row_spec(D_MODEL),
            tr.mod_spec(li, 1),
            tr.mod_spec(li, 0),
            _const_spec((None, 1, D_MODEL), (li, 0, 0)),
            _const_spec((None, D_MODEL, 6 * MIX_W), (li // 2, 0, 0)),
        ],
        out_specs=specs,
        compiler_params=_cparams(1),
        name="proj_odd",
    )(x, tr.mod, tr.mod, g_pre, w_in)


def _outproj_kernel(*refs, n_in, passes):
    a_refs = refs[:n_in]
    w_refs = refs[n_in:2 * n_in]
    x_ref, gate_ref, gp_ref, o_ref = refs[2 * n_in:]
    acc = _dot(_lhs(a_refs[0][...], passes), w_refs[0][...], passes)
    for a_ref, w_ref in zip(a_refs[1:], w_refs[1:]):
        acc = acc + _dot(_lhs(a_ref[...], passes), w_ref[...], passes)
    ms = jnp.mean(acc * acc, axis=-1, keepdims=True)
    y = acc * lax.rsqrt(ms + EPS) * gp_ref[...]
    o_ref[...] = x_ref[...] + gate_ref[...] * y


def _outproj(tr, acts, w, w_idx, x, li, gate_k, g_post):
    t = x.shape[0]
    widths = [a.shape[1] for a in acts]
    assert all(wd == widths[0] for wd in widths)
    w_specs = [pl.BlockSpec((None, widths[0], D_MODEL), lambda i, n=n: (w_idx, n, 0)) for n in range(len(acts))]
    return pl.pallas_call(
        functools.partial(_outproj_kernel, n_in=len(acts), passes=tr.passes),
        out_shape=jax.ShapeDtypeStruct((t, D_MODEL), f32),
        grid=(t // tr.tm,),
        in_specs=[tr.row_spec(wd) for wd in widths] + w_specs + [
            tr.row_spec(D_MODEL),
            tr.mod_spec(li, gate_k),
            _const_spec((None, 1, D_MODEL), (li, 0, 0)),
        ],
        out_specs=tr.row_spec(D_MODEL),
        compiler_params=_cparams(1),
        name="outproj",
    )(*acts, *([w] * len(acts)), x, tr.mod, g_post)


def _retention_kernel(q_ref, k_ref, v_ref, g_ref, intra_ref, qd_ref, kd_ref, gn_ref, o_ref, s_out_ref, s_scr, *, chunk_decay):
    c = pl.program_id(1)

    @pl.when(c == 0)
    def _():
        s_scr[...] = jnp.zeros_like(s_scr)

    heads = range(H_RET)
    sls = [slice(hh * DK_RET, (hh + 1) * DK_RET) for hh in heads]
    s_old = [s_scr[hh] for hh in heads]
    scores = [(_dot_nt(q_ref[:, sls[hh]], k_ref[:, sls[hh]]) * intra_ref[hh]).astype(bf16) for hh in heads]
    cross = [jnp.dot(q_ref[:, sls[hh]], s_old[hh].astype(bf16), preferred_element_type=f32) * qd_ref[hh] for hh in heads]
    kts = [(k_ref[:, sls[hh]].astype(f32) * kd_ref[hh]).T.astype(bf16) for hh in heads]
    outs = [jnp.dot(scores[hh], v_ref[:, sls[hh]], preferred_element_type=f32) + cross[hh] for hh in heads]
    for hh in heads:
        s_scr[hh] = s_old[hh] * chunk_decay[hh] + jnp.dot(kts[hh], v_ref[:, sls[hh]], preferred_element_type=f32)
    for hh in heads:
        sl = sls[hh]
        o = outs[hh]
        mu = jnp.mean(o, axis=-1, keepdims=True)
        d = o - mu
        var = jnp.mean(d * d, axis=-1, keepdims=True)
        on = d * lax.rsqrt(var + EPS) * gn_ref[hh]
        o_ref[:, sl] = (on * _silu(g_ref[:, sl])).astype(o_ref.dtype)

    @pl.when(c == pl.num_programs(1) - 1)
    def _():
        s_out_ref[...] = s_scr[...]


def _retention_tables(c):
    log_g = jnp.log1p(-jnp.exp2(-5.0 - jnp.arange(H_RET, dtype=f32)))
    idx = jnp.arange(c, dtype=f32)
    diff = idx[:, None] - idx[None, :]
    intra = jnp.where(diff >= 0, jnp.exp(jnp.maximum(diff, 0.0)[None] * log_g[:, None, None]), 0.0)
    q_decay = jnp.exp((idx + 1.0)[None, :] * log_g[:, None])[..., None]
    k_decay = jnp.exp((c - 1.0 - idx)[None, :] * log_g[:, None])[..., None]
    qd = jnp.broadcast_to(q_decay, (H_RET, c, DV_RET))
    kd = jnp.broadcast_to(k_decay, (H_RET, c, DK_RET))
    return intra, qd, kd


def _chunk_decay(c):
    return tuple(float(math.exp(c * math.log1p(-(2.0 ** (-5.0 - h))))) for h in range(H_RET))


def _retention(q, k, v, g, gn_g, bsz, seq):
    c = RET_CHUNK
    nc = seq // c
    intra, qd, kd = _retention_tables(c)
    row = pl.BlockSpec((c, MIX_W), lambda b, j: (b * nc + j, 0))
    tab = _const_spec((H_RET, c, c), (0, 0, 0))
    return pl.pallas_call(
        functools.partial(_retention_kernel, chunk_decay=_chunk_decay(c)),
        out_shape=[
            jax.ShapeDtypeStruct((bsz * seq, MIX_W), bf16),
            jax.ShapeDtypeStruct((bsz, H_RET, DK_RET, DV_RET), f32),
        ],
        grid=(bsz, nc),
        in_specs=[row, row, row, row, tab, tab, tab, _const_spec((H_RET, 1, DV_RET), (0, 0, 0))],
        out_specs=[row, pl.BlockSpec((None, H_RET, DK_RET, DV_RET), lambda b, j: (b, 0, 0, 0))],
        scratch_shapes=[pltpu.VMEM((H_RET, DK_RET, DV_RET), f32)],
        compiler_params=_cparams(2),
        name="retention",
    )(q, k, v, g, intra, qd, kd, gn_g.reshape(H_RET, 1, DV_RET))


U_GROUPS = S5_WIDTH // LANES
ST_PER_UG = S5_N // U_GROUPS


def _s5_params(lam_re, lam_im, log_step, b_re, b_im, c_re, c_im):
    dt = jnp.exp(log_step)[:, None]
    mag = jnp.exp(lam_re * dt)
    a_re, a_im = mag * jnp.cos(lam_im * dt), mag * jnp.sin(lam_im * dt)
    den = lam_re * lam_re + lam_im * lam_im
    coef_re = ((a_re - 1.0) * lam_re + a_im * lam_im) / den
    coef_im = (a_im * lam_re - (a_re - 1.0) * lam_im) / den
    bbar_re = coef_re[..., None] * b_re - coef_im[..., None] * b_im
    bbar_im = coef_re[..., None] * b_im + coef_im[..., None] * b_re
    gpb = LANES // S5_GROUP
    eye = jnp.eye(gpb, dtype=f32)

    def b_layout(bb):
        bb = bb.reshape(U_GROUPS, gpb, S5_STATE, S5_GROUP).transpose(0, 1, 3, 2)
        return jnp.einsum("jgip,gh->jgihp", bb, eye).reshape(U_GROUPS, LANES, ST_PER_UG)

    def c_layout(cc):
        cc = cc.reshape(U_GROUPS, gpb, S5_GROUP, S5_STATE)
        return jnp.einsum("jgip,gh->jgphi", cc, eye).reshape(U_GROUPS, ST_PER_UG, LANES)

    bw = jnp.concatenate([b_layout(bbar_re), b_layout(bbar_im)], axis=-1)
    cw = jnp.concatenate([c_layout(c_re), -c_layout(c_im)], axis=1)
    return a_re.reshape(1, S5_N), a_im.reshape(1, S5_N), bw, cw


S5_TS = 128
S5_PITCH = S5_TS + SUBLANES
N_SLAB = S5_N // LANES
SLABS_PER_UG = ST_PER_UG // LANES


def _s5_kernel(u_ref, bw_ref, a_ref, cw_ref, d_ref, glu_ref, y_ref, xr_out, xi_out, xr_scr, xi_scr, car_scr, *, bsz):
    j = pl.program_id(0)
    ts = S5_TS

    @pl.when(j == 0)
    def _():
        car_scr[...] = jnp.zeros_like(car_scr)

    for b in range(bsz):
        rows = slice(b * S5_PITCH, b * S5_PITCH + ts)
        ub = u_ref[:, b * S5_WIDTH:(b + 1) * S5_WIDTH].astype(bf16)
        for g in range(U_GROUPS):
            bu = jnp.dot(ub[:, g * LANES:(g + 1) * LANES], bw_ref[g], preferred_element_type=f32)
            for s in range(SLABS_PER_UG):
                xr_scr[g * SLABS_PER_UG + s, rows, :] = bu[:, s * LANES:(s + 1) * LANES]
                xi_scr[g * SLABS_PER_UG + s, rows, :] = bu[:, ST_PER_UG + s * LANES:ST_PER_UG + (s + 1) * LANES]

    def step(t, carry):
        new = []
        for s in range(N_SLAB):
            xr, xi = carry[2 * s], carry[2 * s + 1]
            ar = a_ref[0, s]
            ai = a_ref[1, s]
            at_t = pl.ds(t, bsz, stride=S5_PITCH)
            nr = (ar * xr - ai * xi) + xr_scr[s, at_t, :]
            ni = (ar * xi + ai * xr) + xi_scr[s, at_t, :]
            xr_scr[s, at_t, :] = nr
            xi_scr[s, at_t, :] = ni
            new += [nr, ni]
        return tuple(new)

    last = lax.fori_loop(0, ts, step, tuple(car_scr[k] for k in range(2 * N_SLAB)))
    for k in range(2 * N_SLAB):
        car_scr[k] = last[k]

    pre = []
    for b in range(bsz):
        rows = slice(b * S5_PITCH, b * S5_PITCH + ts)
        ys = []
        for g in range(U_GROUPS):
            slabs = range(g * SLABS_PER_UG, (g + 1) * SLABS_PER_UG)
            xr_g = jnp.concatenate([xr_scr[s, rows, :] for s in slabs], axis=1).astype(bf16)
            xi_g = jnp.concatenate([xi_scr[s, rows, :] for s in slabs], axis=1).astype(bf16)
            ys.append(jnp.dot(xr_g, cw_ref[g, :ST_PER_UG, :], preferred_element_type=f32)
                      + jnp.dot(xi_g, cw_ref[g, ST_PER_UG:, :], preferred_element_type=f32))
        pre.append(jnp.concatenate(ys, axis=1))
    acts = [_gelu(pre[b] + d_ref[...] * u_ref[:, b * S5_WIDTH:(b + 1) * S5_WIDTH]) for b in range(bsz)]
    gates = [jnp.dot(acts[b].astype(bf16), glu_ref[...], preferred_element_type=f32) for b in range(bsz)]
    for b in range(bsz):
        y_ref[b] = (acts[b] * _sigmoid(gates[b])).astype(y_ref.dtype)

    @pl.when(j == pl.num_programs(0) - 1)
    def _():
        for s in range(N_SLAB):
            xr_out[:, s * LANES:(s + 1) * LANES] = last[2 * s]
            xi_out[:, s * LANES:(s + 1) * LANES] = last[2 * s + 1]


def _s5(u_tb, bw, a_re, a_im, cw, d_skip, w_glu, bsz, seq):
    ts = S5_TS
    a_tab = jnp.stack([a_re, a_im]).reshape(2, N_SLAB, 1, LANES)
    a_tab = jnp.broadcast_to(a_tab, (2, N_SLAB, bsz, LANES))
    st = _const_spec((bsz, S5_N), (0, 0))
    y, xr, xi = pl.pallas_call(
        functools.partial(_s5_kernel, bsz=bsz),
        out_shape=[
            jax.ShapeDtypeStruct((bsz, seq, S5_WIDTH), bf16),
            jax.ShapeDtypeStruct((bsz, S5_N), f32),
            jax.ShapeDtypeStruct((bsz, S5_N), f32),
        ],
        grid=(seq // ts,),
        in_specs=[
            pl.BlockSpec((ts, bsz * S5_WIDTH), lambda j: (j, 0)),
            _const_spec((U_GROUPS, LANES, 2 * ST_PER_UG), (0, 0, 0)),
            _const_spec((2, N_SLAB, bsz, LANES), (0, 0, 0, 0)),
            _const_spec((U_GROUPS, 2 * ST_PER_UG, LANES), (0, 0, 0)),
            _const_spec((1, S5_WIDTH), (0, 0)),
            _const_spec((S5_WIDTH, S5_WIDTH), (0, 0)),
        ],
        out_specs=[pl.BlockSpec((bsz, ts, S5_WIDTH), lambda j: (0, j, 0)), st, st],
        scratch_shapes=[
            pltpu.VMEM((N_SLAB, bsz * S5_PITCH, LANES), f32),
            pltpu.VMEM((N_SLAB, bsz * S5_PITCH, LANES), f32),
            pltpu.VMEM((2 * N_SLAB, bsz, LANES), f32),
        ],
        compiler_params=_cparams(1),
        name="s5",
    )(u_tb, bw, a_tab, cw, d_skip, w_glu)
    return y.reshape(bsz * seq, S5_WIDTH), xr.reshape(bsz, S5_GROUPS, S5_STATE), xi.reshape(bsz, S5_GROUPS, S5_STATE)


FF_CHUNK = 256
FFN_HALO = SUBLANES


def _ffn_up_kernel(x_ref, xh_ref, sc_ref, sh_ref, g_ref, w_ref, cw_ref, cb_ref, act_ref, cs_ref, ext_scr, *, tiles_per_seq):
    i = pl.program_id(0)
    tm = x_ref.shape[0]
    g = g_ref[...]
    sc = sc_ref[...]
    sh = sh_ref[...]
    h = _norm_mod(x_ref[...], g, sc, sh).astype(bf16)
    hh = _norm_mod(xh_ref[...], g, sc, sh).astype(bf16)
    keep = (i % tiles_per_seq != 0).astype(f32)
    for n in range(D_FF // FF_CHUNK):
        mixed = []
        for half in range(2):
            cs = slice(half * D_FF + n * FF_CHUNK, half * D_FF + (n + 1) * FF_CHUNK)
            w = w_ref[:, cs]
            up = jnp.dot(h, w, preferred_element_type=f32)
            halo = jnp.dot(hh, w, preferred_element_type=f32) * keep
            ext_scr[half, 0:FFN_HALO, :] = halo
            ext_scr[half, FFN_HALO:, :] = up
            r1 = ext_scr[half, pl.ds(FFN_HALO - 1, tm), :]
            r2 = ext_scr[half, pl.ds(FFN_HALO - 2, tm), :]
            mixed.append(cw_ref[0:1, cs] * r2 + cw_ref[1:2, cs] * r1 + cw_ref[2:3, cs] * up + cb_ref[:, cs])
            cs_ref[:, cs] = up[tm - (CONV_W - 1):, :]
        act_ref[:, n * FF_CHUNK:(n + 1) * FF_CHUNK] = (mixed[0] * _gelu(mixed[1])).astype(act_ref.dtype)


def _ffn_up(tr, x, li, g_pre, w_up, conv_w, conv_b, bsz):
    t = x.shape[0]
    tm = tr.tm
    tiles = tr.seq // tm
    halo_blocks = tm // FFN_HALO
    return pl.pallas_call(
        functools.partial(_ffn_up_kernel, tiles_per_seq=tiles),
        out_shape=[
            jax.ShapeDtypeStruct((t, D_FF), bf16),
            jax.ShapeDtypeStruct((bsz, CONV_W - 1, 2 * D_FF), f32),
        ],
        grid=(t // tm,),
        in_specs=[
            tr.row_spec(D_MODEL),
            pl.BlockSpec((FFN_HALO, D_MODEL), lambda i: (jnp.maximum(i * halo_blocks - 1, 0), 0)),
            tr.mod_spec(li, 4),
            tr.mod_spec(li, 3),
            _const_spec((None, 1, D_MODEL), (li, 0, 0)),
            _const_spec((None, D_MODEL, 2 * D_FF), (li, 0, 0)),
            _const_spec((None, CONV_W, 2 * D_FF), (li, 0, 0)),
            _const_spec((None, 1, 2 * D_FF), (li, 0, 0)),
        ],
        out_specs=[
            tr.row_spec(D_FF),
            pl.BlockSpec((None, CONV_W - 1, 2 * D_FF), lambda i: (i // tiles, 0, 0)),
        ],
        scratch_shapes=[pltpu.VMEM((2, tm + FFN_HALO, FF_CHUNK), f32)],
        compiler_params=_cparams(1),
        name="ffn_up",
    )(x, x, tr.mod, tr.mod, g_pre, w_up, conv_w, conv_b)


def _t5_bucket(dist):
    n = jnp.maximum(dist, 0)
    max_exact = N_BUCKETS // 2
    large = max_exact + (jnp.log(jnp.maximum(n, 1).astype(f32) / max_exact)
                         / math.log(MAX_DISTANCE / max_exact) * (N_BUCKETS - max_exact)).astype(jnp.int32)
    large = jnp.minimum(large, N_BUCKETS - 1)
    return jnp.where(n < max_exact, n, large)


def _bias_by_distance(rel_bias, n):
    return rel_bias[_t5_bucket(jnp.arange(n))]


ATT_T = 512
SUFFIX_W = 256
SB_HEAD_GROUP = 4


def _diff_tile(q_ref, k_ref, v_ref, bias_ref, m_scr, l_scr, acc_scr, diagonal):
    t = q_ref.shape[0]
    if diagonal:
        causal = lax.broadcasted_iota(jnp.int32, (t, t), 0) >= lax.broadcasted_iota(jnp.int32, (t, t), 1)
    streams = range(2 * H_DIFF)
    scores = []
    for n in streams:
        sl = slice(n * DH_DIFF, (n + 1) * DH_DIFF)
        s = _dot_nt(q_ref[:, sl], k_ref[:, sl]) + bias_ref[n // 2]
        scores.append(jnp.where(causal, s, NEG_BIG) if diagonal else s)
    probs, alphas = [], []
    for n in streams:
        m_old = m_scr[n]
        m_new = jnp.maximum(m_old, jnp.max(scores[n], axis=-1, keepdims=True))
        p = jnp.exp(scores[n] - m_new)
        alpha = jnp.exp(m_old - m_new)
        l_scr[n] = alpha * l_scr[n] + jnp.sum(p, axis=-1, keepdims=True)
        m_scr[n] = m_new
        probs.append(p.astype(bf16))
        alphas.append(alpha)
    for n in streams:
        vh = v_ref[:, (n // 2) * 2 * DH_DIFF:(n // 2 + 1) * 2 * DH_DIFF]
        acc_scr[n] = alphas[n] * acc_scr[n] + jnp.dot(probs[n], vh, preferred_element_type=f32)


def _diff_attn_kernel(lam_ref, q_ref, k_ref, v_ref, bias_ref, sg_ref, o_ref, m_scr, l_scr, acc_scr, *, out_scale):
    qi = pl.program_id(1)
    kj = pl.program_id(2)

    @pl.when(kj == 0)
    def _():
        m_scr[...] = jnp.full_like(m_scr, NEG_BIG)
        l_scr[...] = jnp.zeros_like(l_scr)
        acc_scr[...] = jnp.zeros_like(acc_scr)

    @pl.when(kj < qi)
    def _():
        _diff_tile(q_ref, k_ref, v_ref, bias_ref, m_scr, l_scr, acc_scr, False)

    @pl.when(kj == qi)
    def _():
        _diff_tile(q_ref, k_ref, v_ref, bias_ref, m_scr, l_scr, acc_scr, True)
        lam = lam_ref[0]
        for hh in range(H_DIFF):
            o = acc_scr[2 * hh] / l_scr[2 * hh] - lam * (acc_scr[2 * hh + 1] / l_scr[2 * hh + 1])
            ms = jnp.mean(o * o, axis=-1, keepdims=True)
            o = o * lax.rsqrt(ms + EPS) * sg_ref[...] * out_scale
            o_ref[:, hh * 2 * DH_DIFF:(hh + 1) * 2 * DH_DIFF] = o.astype(o_ref.dtype)


def _diff_attn(q, k, v, bias_tiles, lam, subln_g, lam_init, bsz, seq):
    t = ATT_T
    nq = seq // t
    qspec = pl.BlockSpec((t, MIX_W), lambda b, i, j: (b * nq + i, 0))
    kspec = pl.BlockSpec((t, MIX_W), lambda b, i, j: (b * nq + jnp.minimum(j, i), 0))
    return pl.pallas_call(
        functools.partial(_diff_attn_kernel, out_scale=1.0 - lam_init),
        out_shape=jax.ShapeDtypeStruct((bsz * seq, MIX_W), bf16),
        grid=(bsz, nq, nq),
        in_specs=[
            pl.BlockSpec(memory_space=pltpu.SMEM),
            qspec,
            kspec,
            kspec,
            pl.BlockSpec((H_DIFF, None, t, t), lambda b, i, j: (0, jnp.maximum(i - j, 0), 0, 0)),
            _const_spec((1, 2 * DH_DIFF), (0, 0)),
        ],
        out_specs=qspec,
        scratch_shapes=[
            pltpu.VMEM((2 * H_DIFF, t, 1), f32),
            pltpu.VMEM((2 * H_DIFF, t, 1), f32),
            pltpu.VMEM((2 * H_DIFF, t, 2 * DH_DIFF), f32),
        ],
        compiler_params=_cparams(3),
        name="diff_attn",
    )(lam, q, k, v, bias_tiles, subln_g)


def _log_sigmoid_pair(z):
    ls = jnp.minimum(z, 0.0) - jnp.log(1.0 + jnp.exp(-jnp.abs(z)))
    return ls, ls - z


def _suffix_sum(lk, upper):
    hi, lo = _split(lk)
    return jnp.dot(hi, upper, preferred_element_type=f32) + jnp.dot(lo, upper, preferred_element_type=f32)


def _upper_ones(t):
    j = lax.broadcasted_iota(jnp.int32, (t, t), 0)
    k = lax.broadcasted_iota(jnp.int32, (t, t), 1)
    return jnp.where(j > k, 1.0, 0.0).astype(bf16)


def _sb_attn_kernel(q_ref, k_ref, v_ref, o_ref, r_scr, acc_scr):
    qi = pl.program_id(1)
    st = pl.program_id(2)
    t = q_ref.shape[0]

    @pl.when(st == 0)
    def _():
        r_scr[...] = jnp.zeros_like(r_scr)
        acc_scr[...] = jnp.zeros_like(acc_scr)

    def tile(diagonal):
        if diagonal:
            strict = lax.broadcasted_iota(jnp.int32, (t, t), 1) < lax.broadcasted_iota(jnp.int32, (t, t), 0)
        upper = _upper_ones(SUFFIX_W)
        nblk = t // SUFFIX_W
        for g0 in range(0, H_SB, SB_HEAD_GROUP):
            heads = range(g0, g0 + SB_HEAD_GROUP)
            pairs = {}
            for hh in heads:
                sl = slice(hh * DH_SB, (hh + 1) * DH_SB)
                ls, lk = _log_sigmoid_pair(_dot_nt(q_ref[:, sl], k_ref[:, sl]))
                pairs[hh] = (ls, jnp.where(strict, lk, 0.0) if diagonal else lk)
            sufs = {hh: [_suffix_sum(pairs[hh][1][:, c * SUFFIX_W:(c + 1) * SUFFIX_W], upper) for c in range(nblk)]
                    for hh in heads}
            ws = {}
            for hh in heads:
                ls, lk = pairs[hh]
                carry = r_scr[hh]
                afters = [None] * nblk
                for c in reversed(range(nblk)):
                    afters[c] = sufs[hh][c] + carry
                    carry = carry + jnp.sum(lk[:, c * SUFFIX_W:(c + 1) * SUFFIX_W], axis=-1, keepdims=True)
                r_scr[hh] = carry
                w = jnp.exp(ls + jnp.concatenate(afters, axis=1))
                ws[hh] = (jnp.where(strict, w, 0.0) if diagonal else w).astype(bf16)
            for hh in heads:
                sl = slice(hh * DH_SB, (hh + 1) * DH_SB)
                acc_scr[hh] = acc_scr[hh] + jnp.dot(ws[hh], v_ref[:, sl], preferred_element_type=f32)

    @pl.when(st == 0)
    def _():
        tile(True)

    @pl.when(jnp.logical_and(st > 0, st <= qi))
    def _():
        tile(False)

    @pl.when(st == qi)
    def _():
        for hh in range(H_SB):
            o_ref[:, hh * DH_SB:(hh + 1) * DH_SB] = acc_scr[hh].astype(o_ref.dtype)


def _sb_attn(q, k, v, bsz, seq):
    t = ATT_T
    nq = seq // t
    qspec = pl.BlockSpec((t, MIX_W), lambda b, i, s: (b * nq + i, 0))
    kspec = pl.BlockSpec((t, MIX_W), lambda b, i, s: (b * nq + jnp.maximum(i - s, 0), 0))
    return pl.pallas_call(
        _sb_attn_kernel,
        out_shape=jax.ShapeDtypeStruct((bsz * seq, MIX_W), bf16),
        grid=(bsz, nq, nq),
        in_specs=[qspec, kspec, kspec],
        out_specs=qspec,
        scratch_shapes=[pltpu.VMEM((H_SB, t, 1), f32), pltpu.VMEM((H_SB, t, DH_SB), f32)],
        compiler_params=_cparams(3),
        name="sb_attn",
    )(q, k, v)


DEC_BB = 8


def _even_dec_kernel(q_ref, k_ref, v_ref, g_ref, u_ref, s0_ref, x0r_ref, x0i_ref, gn_ref, bw_ref, are_ref, aim_ref,
                     cw_ref, d_ref, glu_ref, o_ref, y_ref, s_ref, xr_ref, xi_ref, *, decay):
    pad = jnp.zeros((DK_RET - DEC_BB, DK_RET), f32)
    for hh in range(H_RET):
        sl = slice(hh * DK_RET, (hh + 1) * DK_RET)
        qh = q_ref[:, sl]
        kh = k_ref[:, sl]
        vh = v_ref[:, sl]
        q_t = jnp.concatenate([qh, pad], axis=0).T
        k_t = jnp.concatenate([kh, pad], axis=0).T
        qk = jnp.sum(qh * kh, axis=-1, keepdims=True)
        rows = []
        for b in range(DEC_BB):
            s0 = s0_ref[b, hh]
            qs = jnp.sum(q_t[:, b:b + 1] * s0, axis=0, keepdims=True)
            rows.append(qk[b:b + 1] * vh[b:b + 1] + qs * decay[hh])
            s_ref[b, hh] = s0 * decay[hh] + k_t[:, b:b + 1] * vh[b:b + 1]
        o = jnp.concatenate(rows, axis=0)
        mu = jnp.mean(o, axis=-1, keepdims=True)
        dlt = o - mu
        var = jnp.mean(dlt * dlt, axis=-1, keepdims=True)
        on = dlt * lax.rsqrt(var + EPS) * gn_ref[hh]
        o_ref[:, sl] = on * _silu(g_ref[:, sl])

    u = u_ref[...]
    bus = [_dot(_lhs(u[:, j * LANES:(j + 1) * LANES], 3), bw_ref[j], 3) for j in range(U_GROUPS)]
    bu_re = jnp.concatenate([b[:, :ST_PER_UG] for b in bus], axis=1)
    bu_im = jnp.concatenate([b[:, ST_PER_UG:] for b in bus], axis=1)
    are = are_ref[...]
    aim = aim_ref[...]
    x0r = x0r_ref[...]
    x0i = x0i_ref[...]
    xr = (are * x0r - aim * x0i) + bu_re
    xi = (are * x0i + aim * x0r) + bu_im
    xr_ref[...] = xr
    xi_ref[...] = xi
    ys = []
    for j in range(U_GROUPS):
        ss = slice(j * ST_PER_UG, (j + 1) * ST_PER_UG)
        ys.append(_dot(_lhs(xr[:, ss], 3), cw_ref[j, :ST_PER_UG, :], 3) + _dot(_lhs(xi[:, ss], 3), cw_ref[j, ST_PER_UG:, :], 3))
    y = _gelu(jnp.concatenate(ys, axis=1) + d_ref[...] * u)
    z = _dot(_lhs(y, 3), glu_ref[...], 3)
    y_ref[...] = y * _sigmoid(z)


def _even_dec(q, k, v, g, u, state_ret, s5_re, s5_im, ie, gn_g, bw, a_re, a_im, cw, d_skip, w_glu):
    nb = q.shape[0]
    row = pl.BlockSpec((DEC_BB, MIX_W), lambda i: (i, 0))
    st_spec = pl.BlockSpec((None, DEC_BB, H_RET, DK_RET, DV_RET), lambda i: (ie, i, 0, 0, 0))
    x_spec = pl.BlockSpec((None, DEC_BB, S5_N), lambda i: (ie, i, 0))
    decay = tuple(float(math.exp(math.log1p(-(2.0 ** (-5.0 - h))))) for h in range(H_RET))
    return pl.pallas_call(
        functools.partial(_even_dec_kernel, decay=decay),
        out_shape=[
            jax.ShapeDtypeStruct((nb, MIX_W), f32),
            jax.ShapeDtypeStruct((nb, S5_WIDTH), f32),
            jax.ShapeDtypeStruct((nb, H_RET, DK_RET, DV_RET), f32),
            jax.ShapeDtypeStruct((nb, S5_N), f32),
            jax.ShapeDtypeStruct((nb, S5_N), f32),
        ],
        grid=(nb // DEC_BB,),
        in_specs=[
            row, row, row, row, row, st_spec, x_spec, x_spec,
            _const_spec((H_RET, 1, DV_RET), (0, 0, 0)),
            _const_spec((U_GROUPS, LANES, 2 * ST_PER_UG), (0, 0, 0)),
            _const_spec((1, S5_N), (0, 0)),
            _const_spec((1, S5_N), (0, 0)),
            _const_spec((U_GROUPS, 2 * ST_PER_UG, LANES), (0, 0, 0)),
            _const_spec((1, S5_WIDTH), (0, 0)),
            _const_spec((S5_WIDTH, S5_WIDTH), (0, 0)),
        ],
        out_specs=[
            row, row,
            pl.BlockSpec((DEC_BB, H_RET, DK_RET, DV_RET), lambda i: (i, 0, 0, 0)),
            pl.BlockSpec((DEC_BB, S5_N), lambda i: (i, 0)),
            pl.BlockSpec((DEC_BB, S5_N), lambda i: (i, 0)),
        ],
        compiler_params=_cparams(1),
        name="even_dec",
    )(q, k, v, g, u, state_ret, s5_re, s5_im, gn_g.reshape(H_RET, 1, DV_RET), bw, a_re, a_im, cw, d_skip, w_glu)


def _ffn_up_dec_kernel(x_ref, sc_ref, sh_ref, g_ref, wv_ref, wg_ref, b0v_ref, b0g_ref, b1v_ref, b1g_ref,
                       cwv_ref, cwg_ref, cbv_ref, cbg_ref, act_ref, upv_ref, upg_ref):
    h = _lhs(_norm_mod(x_ref[...], g_ref[...], sc_ref[...], sh_ref[...]), 3)
    upv = _dot(h, wv_ref[...], 3)
    upg = _dot(h, wg_ref[...], 3)
    val = cwv_ref[0:1, :] * b0v_ref[...] + cwv_ref[1:2, :] * b1v_ref[...] + cwv_ref[2:3, :] * upv + cbv_ref[...]
    gate = cwg_ref[0:1, :] * b0g_ref[...] + cwg_ref[1:2, :] * b1g_ref[...] + cwg_ref[2:3, :] * upg + cbg_ref[...]
    act_ref[...] = val * _gelu(gate)
    upv_ref[...] = upv
    upg_ref[...] = upg


def _ffn_up_dec(x, mod_s, li, g_pre, w_up, conv_w, conv_b, state_conv2d):
    nb = x.shape[0]
    nch = D_FF // FF_CHUNK
    full = _const_spec((nb, D_MODEL), (0, 0))

    def mod(k):
        return pl.BlockSpec((None, nb, D_MODEL), lambda n: (li, 0, k))

    def cols(rows, off):
        return pl.BlockSpec((None, rows, FF_CHUNK), lambda n: (li, 0, n + off))

    out = pl.BlockSpec((nb, FF_CHUNK), lambda n: (0, n))
    return pl.pallas_call(
        _ffn_up_dec_kernel,
        out_shape=[jax.ShapeDtypeStruct((nb, D_FF), f32)] * 3,
        grid=(nch,),
        in_specs=[
            full, mod(4), mod(3), _const_spec((None, 1, D_MODEL), (li, 0, 0)),
            cols(D_MODEL, 0), cols(D_MODEL, nch),
            cols(nb, 0), cols(nb, nch), cols(nb, 2 * nch), cols(nb, 3 * nch),
            cols(CONV_W, 0), cols(CONV_W, nch), cols(1, 0), cols(1, nch),
        ],
        out_specs=[out, out, out],
        compiler_params=_cparams(1),
        name="ffn_up_dec",
    )(x, mod_s, mod_s, g_pre, w_up, w_up, state_conv2d, state_conv2d, state_conv2d, state_conv2d,
      conv_w, conv_w, conv_b, conv_b)


DEC_PP = 8


def _attn_dec_kernel(pt_ref, lam_ref, qd_ref, qs_ref, kn_ref, vn_ref, bias_ref, bias0_ref, sg_ref, *rest, n_pages, out_scale):
    del pt_ref
    pages = rest[:4 * DEC_PP]
    o_ref, md, ld, accd, rs, accs = rest[4 * DEC_PP:]
    j = pl.program_id(1)
    rows = H_SB
    hd = 2 * DH_DIFF
    head_of_lane = lax.broadcasted_iota(jnp.int32, (rows, MIX_W), 1) // DH_SB
    q_sb = jnp.where(head_of_lane == lax.broadcasted_iota(jnp.int32, (rows, MIX_W), 0), qs_ref[...], 0.0)

    def per_row_head(ref):
        return jnp.concatenate([ref[:, (r // 2) * hd:(r // 2 + 1) * hd] for r in range(rows)], axis=0)

    half_of_lane = lax.broadcasted_iota(jnp.int32, (rows, hd), 1) // DH_DIFF
    row_id = lax.broadcasted_iota(jnp.int32, (rows, hd), 0)
    q_diff = jnp.where(half_of_lane == row_id % 2, per_row_head(qd_ref), 0.0)

    @pl.when(j == 0)
    def _():
        md[...] = jnp.sum(q_diff * per_row_head(kn_ref), axis=-1, keepdims=True) + bias0_ref[...]
        ld[...] = jnp.ones_like(ld)
        accd[...] = per_row_head(vn_ref)
        rs[...] = jnp.zeros_like(rs)
        accs[...] = jnp.zeros_like(accs)

    qd_b = q_diff.astype(bf16)
    qs_b = q_sb.astype(bf16)
    upper = _upper_ones(PAGE_SIZE)

    flat = PAGE_SIZE * H_DIFF
    own = lax.broadcasted_iota(jnp.int32, (rows, flat), 1) % H_DIFF == lax.broadcasted_iota(jnp.int32, (rows, flat), 0) // 2
    scores = []
    for i in range(DEC_PP):
        page = n_pages - 1 - (j * DEC_PP + i)
        s = _dot_nt(qd_b, pages[4 * i][...].astype(bf16)) + bias_ref[page]
        scores.append(jnp.where(own, s, NEG_BIG))
    m_old = md[...]
    m_new = m_old
    for s in scores:
        m_new = jnp.maximum(m_new, jnp.max(s, axis=-1, keepdims=True))
    alpha = jnp.exp(m_old - m_new)
    l_new = alpha * ld[...]
    acc_d = alpha * accd[...]
    for i, s in enumerate(scores):
        p = jnp.exp(s - m_new)
        l_new = l_new + jnp.sum(p, axis=-1, keepdims=True)
        acc_d = acc_d + jnp.dot(p.astype(bf16), pages[4 * i + 1][...].astype(bf16), preferred_element_type=f32)
    ld[...] = l_new
    md[...] = m_new
    accd[...] = acc_d

    zs = [jnp.dot(qs_b, pages[4 * i + 2][...].reshape(MIX_W, PAGE_SIZE).astype(bf16), preferred_element_type=f32)
          for i in range(DEC_PP)]
    pairs = [_log_sigmoid_pair(z) for z in zs]
    sufs = [_suffix_sum(lk, upper) for _, lk in pairs]
    carry = rs[...]
    ws = []
    for (ls, lk), suf in zip(pairs, sufs):
        ws.append(jnp.exp(ls + (suf + carry)).astype(bf16))
        carry = carry + jnp.sum(lk, axis=-1, keepdims=True)
    acc_s = accs[...]
    for i, w in enumerate(ws):
        acc_s = acc_s + _dot_nt(w, pages[4 * i + 3][...].reshape(MIX_W, PAGE_SIZE).astype(bf16))
    rs[...] = carry
    accs[...] = acc_s

    @pl.when(j == pl.num_programs(1) - 1)
    def _():
        lam = lam_ref[0]
        acc = accd[...] / ld[...]
        pieces = []
        for hh in range(H_DIFF):
            o = acc[2 * hh:2 * hh + 1] - lam * acc[2 * hh + 1:2 * hh + 2]
            ms = jnp.mean(o * o, axis=-1, keepdims=True)
            pieces.append(o * lax.rsqrt(ms + EPS) * sg_ref[...] * out_scale)
        a_sb = accs[...]
        for hh in range(H_SB):
            pieces.append(a_sb[hh:hh + 1, hh * DH_SB:(hh + 1) * DH_SB])
        o_ref[...] = jnp.concatenate(pieces, axis=1)


def _attn_dec(page_table, lam, qd, qs, k_new, v_new, bias_pages, bias0, subln_g, caches, io, lam_init):
    nb, n_pages = page_table.shape
    row3 = lambda a: a.reshape(nb, 1, a.shape[-1])
    rspec = pl.BlockSpec((None, 1, MIX_W), lambda b, j, pt: (b, 0, 0))

    def page_spec(i, cache):
        blk = (None, None) + cache.shape[2:]
        zeros = (0,) * (cache.ndim - 2)
        return pl.BlockSpec(blk, lambda b, j, pt: (io, pt[b, n_pages - 1 - (j * DEC_PP + i)]) + zeros)

    page_specs, page_args = [], []
    for i in range(DEC_PP):
        for cache in caches:
            page_specs.append(page_spec(i, cache))
            page_args.append(cache)
    grid_spec = pltpu.PrefetchScalarGridSpec(
        num_scalar_prefetch=1,
        grid=(nb, n_pages // DEC_PP),
        in_specs=[
            pl.BlockSpec(memory_space=pltpu.SMEM),
            rspec, rspec, rspec, rspec,
            pl.BlockSpec((n_pages, 1, PAGE_SIZE * H_DIFF), lambda b, j, pt: (0, 0, 0)),
            pl.BlockSpec((H_SB, 1), lambda b, j, pt: (0, 0)),
            pl.BlockSpec((1, 2 * DH_DIFF), lambda b, j, pt: (0, 0)),
        ] + page_specs,
        out_specs=pl.BlockSpec((None, 1, 2 * MIX_W), lambda b, j, pt: (b, 0, 0)),
        scratch_shapes=[
            pltpu.VMEM((H_SB, 1), f32), pltpu.VMEM((H_SB, 1), f32), pltpu.VMEM((H_SB, 2 * DH_DIFF), f32),
            pltpu.VMEM((H_SB, 1), f32), pltpu.VMEM((H_SB, MIX_W), f32),
        ],
    )
    out = pl.pallas_call(
        functools.partial(_attn_dec_kernel, n_pages=n_pages, out_scale=1.0 - lam_init),
        out_shape=jax.ShapeDtypeStruct((nb, 1, 2 * MIX_W), f32),
        grid_spec=grid_spec,
        compiler_params=_cparams(2),
        name="attn_dec",
    )(page_table, lam, row3(qd), row3(qs), row3(k_new), row3(v_new), bias_pages, bias0, subln_g, *page_args)
    return out.reshape(nb, 2 * MIX_W)


def _lam_init(li):
    return 0.8 - 0.6 * math.exp(-0.3 * li)


def _diff_lambda(lam_vecs, li):
    lv = lam_vecs.astype(f32)
    lam = jnp.exp(jnp.dot(lv[0], lv[1])) - jnp.exp(jnp.dot(lv[2], lv[3])) + _lam_init(li)
    return lam.reshape(1)


def _rope_tables(pos):
    half = DK_RET // 2
    inv = jnp.power(ROPE_BASE, -jnp.arange(half, dtype=f32) / half)
    ang = pos.astype(f32)[:, None] * inv[None, :]
    cos, sin = jnp.cos(ang), jnp.sin(ang)
    return jnp.concatenate([cos, cos], axis=-1), jnp.concatenate([-sin, sin], axis=-1)


def _prompt_bias_tiles(bias_d, seq):
    t = ATT_T
    nd = seq // t
    m = jnp.arange(2 * t)
    r_minus_c = jnp.where(m < t, -m, 2 * t - m)
    dist = jnp.arange(nd)[:, None] * t + r_minus_c[None, :]
    gen = jnp.transpose(bias_d[jnp.clip(dist, 0, seq - 1)], (2, 0, 1))
    rows = jnp.tile(gen, (1, 1, t))[:, :, :t * (2 * t - 1)].reshape(gen.shape[0], nd, t, 2 * t - 1)
    return rows[:, :, :, :t]


def _prompt_trunk(x_prompt, mod_p, prm):
    bsz, seq, _ = x_prompt.shape
    tm = min(512, seq)
    tr = _Trunk(mod_p, seq, tm, 1)
    x = x_prompt.reshape(bsz * seq, D_MODEL)
    cos2, sin2 = _rope_tables(jnp.arange(seq))
    bias_tiles = _prompt_bias_tiles(_bias_by_distance(prm["rel_bias"], seq), seq)
    new = {k: [] for k in ("ret", "s5r", "s5i", "kd", "vd", "ks", "vs", "conv")}
    for li in range(DEPTH):
        if li % 2 == 0:
            ie = li // 2
            s5p = prm["s5"][ie]
            q, k, v, g, u = _proj_even(tr, x, li, prm["g_pre_mix"], prm["w_in_even_b"], cos2, sin2, (bf16, bf16, bf16, f32, f32))
            o, ret_s = _retention(q, k, v, g, prm["ret_gn_g"][ie], bsz, seq)
            y, xr, xi = _s5(u, s5p["bw"].astype(bf16), s5p["a_re"], s5p["a_im"], s5p["cw"].astype(bf16), s5p["d"],
                            s5p["glu_b"], bsz, seq)
            new["ret"].append(ret_s)
            new["s5r"].append(xr)
            new["s5i"].append(xi)
            x = _outproj(tr, [o, y], prm["w_out_even_b"], ie, x, li, 2, prm["g_post_mix"])
        else:
            io = li // 2
            dq, dk, dv, sq, sk, sv, dk_b, dv_b, sk_b, sv_b = _proj_odd(tr, x, li, prm["g_pre_mix"], prm["w_in_odd_b"], True)
            lam = _diff_lambda(prm["diff_lam"][io], li)
            od = _diff_attn(dq, dk_b, dv_b, bias_tiles, lam, prm["diff_subln_g"][io].reshape(1, -1), _lam_init(li), bsz, seq)
            osb = _sb_attn(sq, sk_b, sv_b, bsz, seq)
            new["kd"].append(dk.reshape(bsz, seq, H_DIFF, 2 * DH_DIFF))
            new["vd"].append(dv.reshape(bsz, seq, H_DIFF, 2 * DH_DIFF))
            new["ks"].append(sk.reshape(bsz, seq, H_SB, DH_SB))
            new["vs"].append(sv.reshape(bsz, seq, H_SB, DH_SB))
            x = _outproj(tr, [od, osb], prm["w_out_odd_b"], io, x, li, 2, prm["g_post_mix"])
        act, cs = _ffn_up(tr, x, li, prm["g_pre_ffn"], prm["w_up_b"], prm["conv_w"], prm["conv_b"], bsz)
        new["conv"].append(cs)
        x = _outproj(tr, [act], prm["w_down_b"], li, x, li, 5, prm["g_post_ffn"])
    st = lambda t: jnp.stack(t, axis=0)
    return x.reshape(bsz, seq, D_MODEL), tuple(st(new[k]) for k in ("ret", "s5r", "s5i", "kd", "vd", "ks", "vs", "conv"))


def _sample_trunk(x_sample, mod_s, prm, past):
    nb = x_sample.shape[0]
    tr = _Trunk(mod_s, 1, nb, 3)
    x = x_sample.reshape(nb, D_MODEL)
    page_table = past["page_table"]
    n_pages = page_table.shape[1]
    q_off = n_pages * PAGE_SIZE
    cos2, sin2 = _rope_tables(jnp.full((1,), q_off))
    bias_d = _bias_by_distance(prm["rel_bias"], q_off + 1)
    key_pos = jnp.arange(q_off).reshape(n_pages, PAGE_SIZE)
    bias_pages = bias_d[q_off - key_pos].reshape(n_pages, 1, PAGE_SIZE * H_DIFF)
    bias0 = jnp.repeat(bias_d[0], 2).reshape(2 * H_DIFF, 1)
    s5_re = past["s5_re"].reshape(past["s5_re"].shape[0], nb, S5_N)
    s5_im = past["s5_im"].reshape(past["s5_im"].shape[0], nb, S5_N)
    conv2d = past["conv"].reshape(DEPTH, nb, (CONV_W - 1) * 2 * D_FF)
    caches = tuple(past[k].reshape(past[k].shape[0], past[k].shape[1], PAGE_SIZE * H_DIFF, 2 * DH_DIFF)
                   for k in ("diff_k", "diff_v"))
    caches += tuple(jnp.transpose(past[k], (0, 1, 3, 4, 2)) for k in ("sb_k", "sb_v"))
    new = {k: [] for k in ("ret", "s5r", "s5i", "kd", "vd", "ks", "vs", "conv")}
    for li in range(DEPTH):
        if li % 2 == 0:
            ie = li // 2
            s5p = prm["s5"][ie]
            q, k, v, g, u = _proj_even(tr, x, li, prm["g_pre_mix"], prm["w_in_even"], cos2, sin2, (f32,) * 5)
            o, y, ret_s, xr, xi = _even_dec(q, k, v, g, u, past["ret"], s5_re, s5_im, ie, prm["ret_gn_g"][ie],
                                            s5p["bw"], s5p["a_re"], s5p["a_im"], s5p["cw"], s5p["d"], s5p["glu"])
            new["ret"].append(ret_s)
            new["s5r"].append(xr.reshape(nb, S5_GROUPS, S5_STATE))
            new["s5i"].append(xi.reshape(nb, S5_GROUPS, S5_STATE))
            x = _outproj(tr, [o, y], prm["w_out_even"], ie, x, li, 2, prm["g_post_mix"])
        else:
            io = li // 2
            dq, dk, dv, sq, sk, sv = _proj_odd(tr, x, li, prm["g_pre_mix"], prm["w_in_odd"], False)
            lam = _diff_lambda(prm["diff_lam"][io], li)
            mix = _attn_dec(page_table, lam, dq, sq, dk, dv, bias_pages, bias0, prm["diff_subln_g"][io].reshape(1, -1),
                            caches, io, _lam_init(li))
            new["kd"].append(dk.reshape(nb, 1, H_DIFF, 2 * DH_DIFF))
            new["vd"].append(dv.reshape(nb, 1, H_DIFF, 2 * DH_DIFF))
            new["ks"].append(sk.reshape(nb, 1, H_SB, DH_SB))
            new["vs"].append(sv.reshape(nb, 1, H_SB, DH_SB))
            x = _outproj(tr, [mix[:, :MIX_W], mix[:, MIX_W:]], prm["w_out_odd"], io, x, li, 2, prm["g_post_mix"])
        act, upv, upg = _ffn_up_dec(x, mod_s, li, prm["g_pre_ffn"], prm["w_up"], prm["conv_w"], prm["conv_b"], conv2d)
        up = jnp.concatenate([upv, upg], axis=-1)
        new["conv"].append(jnp.stack([past["conv"][li][:, CONV_W - 2], up], axis=1))
        x = _outproj(tr, [act], prm["w_down"], li, x, li, 5, prm["g_post_ffn"])
    st = lambda t: jnp.stack(t, axis=0)
    return x.reshape(nb, 1, D_MODEL), tuple(st(new[k]) for k in ("ret", "s5r", "s5i", "kd", "vd", "ks", "vs", "conv"))


def _prepare_params(g_pre_mix, g_post_mix, g_pre_ffn, g_post_ffn, w_in_even, w_out_even, ret_gn_g, s5_lam_re, s5_lam_im,
                    s5_log_step, s5_b_re, s5_b_im, s5_c_re, s5_c_im, s5_d, s5_w_glu, w_in_odd, w_out_odd, diff_lam,
                    diff_subln_g, rel_bias, w_up, conv_w, conv_b, w_down):
    g3 = lambda g: g.reshape(g.shape[0], 1, g.shape[1])
    prm = dict(
        g_pre_mix=g3(g_pre_mix), g_post_mix=g3(g_post_mix), g_pre_ffn=g3(g_pre_ffn), g_post_ffn=g3(g_post_ffn),
        w_in_even=w_in_even, w_out_even=w_out_even, w_in_odd=w_in_odd, w_out_odd=w_out_odd, w_up=w_up, w_down=w_down,
        w_in_even_b=w_in_even.astype(bf16), w_out_even_b=w_out_even.astype(bf16), w_in_odd_b=w_in_odd.astype(bf16),
        w_out_odd_b=w_out_odd.astype(bf16), w_up_b=w_up.astype(bf16), w_down_b=w_down.astype(bf16),
        ret_gn_g=ret_gn_g, diff_lam=diff_lam, diff_subln_g=diff_subln_g, rel_bias=rel_bias,
        conv_w=conv_w, conv_b=g3(conv_b),
    )
    s5 = []
    for ie in range(s5_lam_re.shape[0]):
        a_re, a_im, bw, cw = _s5_params(s5_lam_re[ie], s5_lam_im[ie], s5_log_step[ie], s5_b_re[ie], s5_b_im[ie],
                                        s5_c_re[ie], s5_c_im[ie])
        s5.append(dict(a_re=a_re, a_im=a_im, bw=bw, cw=cw,
                       d=s5_d[ie].reshape(1, S5_WIDTH), glu=s5_w_glu[ie], glu_b=s5_w_glu[ie].astype(bf16)))
    prm["s5"] = s5
    return prm


def kernel(x_prompt, x_sample, c_prompt, c_sample, state_ret, state_s5_re, state_s5_im, cache_diff_k, cache_diff_v, cache_sb_k, cache_sb_v, state_conv, page_table, w_ada, b_ada, g_pre_mix, g_post_mix, g_pre_ffn, g_post_ffn, w_in_even, w_out_even, ret_gn_g, s5_lam_re, s5_lam_im, s5_log_step, s5_b_re, s5_b_im, s5_c_re, s5_c_im, s5_d, s5_w_glu, w_in_odd, w_out_odd, diff_lam, diff_subln_g, rel_bias, w_up, conv_w, conv_b, w_down):
    nb = x_sample.shape[0]
    bsz = x_prompt.shape[0]
    prm = _prepare_params(g_pre_mix, g_post_mix, g_pre_ffn, g_post_ffn, w_in_even, w_out_even, ret_gn_g, s5_lam_re,
                          s5_lam_im, s5_log_step, s5_b_re, s5_b_im, s5_c_re, s5_c_im, s5_d, s5_w_glu, w_in_odd, w_out_odd,
                          diff_lam, diff_subln_g, rel_bias, w_up, conv_w, conv_b, w_down)
    mod = _ada(jnp.concatenate([c_sample, c_prompt], axis=0), w_ada, b_ada)
    mod_s = mod[:, :nb]
    mod_p = mod[:, nb:].reshape(DEPTH, bsz, 1, 6 * D_MODEL)
    past = dict(ret=state_ret, s5_re=state_s5_re, s5_im=state_s5_im, diff_k=cache_diff_k, diff_v=cache_diff_v,
                sb_k=cache_sb_k, sb_v=cache_sb_v, conv=state_conv, page_table=page_table)
    y_prompt, sp = _prompt_trunk(x_prompt, mod_p, prm)
    y_sample, ss = _sample_trunk(x_sample, mod_s, prm, past)
    ret_p, s5r_p, s5i_p, kd_p, vd_p, ks_p, vs_p, conv_p = sp
    ret_s, s5r_s, s5i_s, kd_s, vd_s, ks_s, vs_s, conv_s = ss
    return (y_prompt, y_sample, ret_p, ret_s, s5r_p, s5r_s, s5i_p, s5i_s, kd_p, kd_s, vd_p, vd_s,
            ks_p, ks_s, vs_p, vs_s, conv_p, conv_s)
```

```python
import functools
import math

import jax
import jax.numpy as jnp
from jax import lax
from jax.experimental import pallas as pl
from jax.experimental.pallas import tpu as pltpu

f32 = jnp.float32
bf16 = jnp.bfloat16

D_MODEL = 1024
DEPTH = 4
PAST_LEN = 8192
PAGE_SIZE = 128
H_RET = 4
DK_RET = 128
DV_RET = 128
RET_CHUNK = 128
ROPE_BASE = 10000.0
S5_WIDTH = D_MODEL // 2
S5_GROUP = 16
S5_GROUPS = S5_WIDTH // S5_GROUP
S5_STATE = 64
S5_N = S5_GROUPS * S5_STATE
H_DIFF = 4
DH_DIFF = 64
H_SB = 8
DH_SB = 64
N_BUCKETS = 32
MAX_DISTANCE = 128
D_FF = 2816
CONV_W = 3
EPS = 1e-6
MIX_W = 512

LANES = 128
SUBLANES = 8
VMEM_LIMIT = 56 * 1024 * 1024
NEG_BIG = -1e30


def _cparams(n_axes):
    return pltpu.CompilerParams(dimension_semantics=("arbitrary",) * n_axes, vmem_limit_bytes=VMEM_LIMIT)


def _split(a):
    hi = a.astype(bf16)
    lo = (a - hi.astype(f32)).astype(bf16)
    return hi, lo


def _lhs(a, passes):
    if passes == 1:
        return (a.astype(bf16),)
    return _split(a.astype(f32))


def _dot(lhs, b, passes):
    if passes == 1:
        return jnp.dot(lhs[0], b.astype(bf16), preferred_element_type=f32)
    b_hi, b_lo = _split(b.astype(f32))
    return jnp.dot(lhs[0], b_hi, preferred_element_type=f32) + (
        jnp.dot(lhs[0], b_lo, preferred_element_type=f32) + jnp.dot(lhs[1], b_hi, preferred_element_type=f32)
    )


def _dot_nt(a, b):
    return lax.dot_general(a, b, (((1,), (1,)), ((), ())), preferred_element_type=f32)


def _norm_mod(x, g, sc, sh):
    ms = jnp.mean(x * x, axis=-1, keepdims=True)
    return (x * lax.rsqrt(ms + EPS) * g) * (1.0 + sc) + sh


def _gelu(x):
    return 0.5 * x * (1.0 + jnp.tanh(math.sqrt(2.0 / math.pi) * (x + 0.044715 * (x * x * x))))


def _sigmoid(x):
    return 1.0 / (1.0 + jnp.exp(-x))


def _silu(x):
    return x * _sigmoid(x)


def _ada_kernel(c_ref, w_ref, b_ref, o_ref):
    s = _silu(c_ref[...])
    o_ref[...] = _dot(_lhs(s, 3), w_ref[...], 3) + b_ref[...]


def _ada(c_all, w_ada, b_ada):
    n = c_all.shape[0]
    tn = 1024
    return pl.pallas_call(
        _ada_kernel,
        out_shape=jax.ShapeDtypeStruct((DEPTH, n, 6 * D_MODEL), f32),
        grid=(DEPTH, 6 * D_MODEL // tn),
        in_specs=[
            pl.BlockSpec((n, D_MODEL), lambda l, j: (0, 0)),
            pl.BlockSpec((None, D_MODEL, tn), lambda l, j: (l, 0, j)),
            pl.BlockSpec((None, 1, tn), lambda l, j: (l, 0, j)),
        ],
        out_specs=pl.BlockSpec((None, n, tn), lambda l, j: (l, 0, j)),
        compiler_params=_cparams(2),
        name="ada_mod",
    )(c_all, w_ada, b_ada.reshape(DEPTH, 1, 6 * D_MODEL))


class _Trunk:
    def __init__(self, mod, seq, tm, passes):
        self.mod = mod
        self.seq = seq
        self.tm = tm
        self.passes = passes
        self.per_row = seq == 1

    def mod_spec(self, li, k):
        if self.per_row:
            return pl.BlockSpec((None, self.tm, D_MODEL), lambda i: (li, i, k))
        tiles_per_seq = self.seq // self.tm
        return pl.BlockSpec((None, None, 1, D_MODEL), lambda i: (li, i // tiles_per_seq, 0, k))

    def row_spec(self, width):
        return pl.BlockSpec((self.tm, width), lambda i: (i, 0))


def _const_spec(shape, idx):
    return pl.BlockSpec(shape, lambda *_: idx)


def _rope(a, cos2, sin2):
    return a * cos2 + pltpu.roll(a, DK_RET // 2, 1) * sin2


def _proj_even_kernel(x_ref, sc_ref, sh_ref, g_ref, w_ref, cos_ref, sin_ref, q_ref, k_ref, v_ref, gg_ref, u_ref, *, passes):
    h = _lhs(_norm_mod(x_ref[...], g_ref[...], sc_ref[...], sh_ref[...]), passes)
    cos2 = cos_ref[...]
    sin2 = sin_ref[...]
    outs = (q_ref, k_ref, v_ref, gg_ref, u_ref)
    for n, ref in enumerate(outs):
        acc = _dot(h, w_ref[:, n * MIX_W:(n + 1) * MIX_W], passes)
        if n < 2:
            scale = 1.0 if n == 0 else DK_RET ** -0.5
            for hh in range(H_RET):
                sl = slice(hh * DK_RET, (hh + 1) * DK_RET)
                ref[:, sl] = (_rope(acc[:, sl], cos2, sin2) * scale).astype(ref.dtype)
        else:
            ref[...] = acc.astype(ref.dtype)


def _proj_even(tr, x, li, g_pre, w_in, cos2, sin2, out_dtypes):
    t = x.shape[0]
    tm = tr.tm
    n_in = w_in.shape[-1]
    if tr.per_row:
        cs_spec = pl.BlockSpec((1, DK_RET), lambda i: (0, 0))
    else:
        tiles = tr.seq // tm
        cs_spec = pl.BlockSpec((tm, DK_RET), lambda i: (i % tiles, 0))
    out_shapes = [jax.ShapeDtypeStruct((t, MIX_W), dt) for dt in out_dtypes]
    out_specs = [tr.row_spec(MIX_W) for _ in out_dtypes]
    if not tr.per_row:
        out_shapes[4] = jax.ShapeDtypeStruct((tr.seq, (t // tr.seq) * MIX_W), out_dtypes[4])
        out_specs[4] = pl.BlockSpec((tm, MIX_W), lambda i: (i % tiles, i // tiles))
    return pl.pallas_call(
        functools.partial(_proj_even_kernel, passes=tr.passes),
        out_shape=out_shapes,
        grid=(t // tm,),
        in_specs=[
            tr.row_spec(D_MODEL),
            tr.mod_spec(li, 1),
            tr.mod_spec(li, 0),
            _const_spec((None, 1, D_MODEL), (li, 0, 0)),
            _const_spec((None, D_MODEL, n_in), (li // 2, 0, 0)),
            cs_spec,
            cs_spec,
        ],
        out_specs=out_specs,
        compiler_params=_cparams(1),
        name="proj_even",
    )(x, tr.mod, tr.mod, g_pre, w_in, cos2, sin2)


def _proj_odd_kernel(x_ref, sc_ref, sh_ref, g_ref, w_ref, *out_refs, passes, with_copies):
    h = _lhs(_norm_mod(x_ref[...], g_ref[...], sc_ref[...], sh_ref[...]), passes)
    for n in range(6):
        acc = _dot(h, w_ref[:, n * MIX_W:(n + 1) * MIX_W], passes)
        if n in (0, 3):
            acc = acc * (DH_DIFF ** -0.5 if n == 0 else DH_SB ** -0.5)
        if with_copies and n in (1, 2):
            tm = acc.shape[0]
            hd = 2 * DH_DIFF
            for hh in range(H_DIFF):
                out_refs[n][pl.ds(hh, tm, stride=H_DIFF), :] = acc[:, hh * hd:(hh + 1) * hd]
        else:
            out_refs[n][...] = acc.astype(out_refs[n].dtype)
        if with_copies and n not in (0, 3):
            out_refs[6 + (n if n < 3 else n - 1) - 1][...] = acc.astype(bf16)


def _proj_odd(tr, x, li, g_pre, w_in, with_copies):
    t = x.shape[0]
    q_dt = bf16 if with_copies else f32
    dts = [q_dt, f32, f32, q_dt, f32, f32] + ([bf16] * 4 if with_copies else [])
    shapes = [(t, MIX_W)] * len(dts)
    specs = [tr.row_spec(MIX_W) for _ in dts]
    if with_copies:
        for n in (1, 2):
            shapes[n] = (t * H_DIFF, 2 * DH_DIFF)
            specs[n] = pl.BlockSpec((tr.tm * H_DIFF, 2 * DH_DIFF), lambda i: (i, 0))
    return pl.pallas_call(
        functools.partial(_proj_odd_kernel, passes=tr.passes, with_copies=with_copies),
        out_shape=[jax.ShapeDtypeStruct(s, dt) for s, dt in zip(shapes, dts)],
        grid=(t // tr.tm,),
        in_specs=[
            tr.row_spec(D_MODEL),
            tr.mod_spec(li, 1),
            tr.mod_spec(li, 0),
            _const_spec((None, 1, D_MODEL), (li, 0, 0)),
            _const_spec((None, D_MODEL, 6 * MIX_W), (li // 2, 0, 0)),
        ],
        out_specs=specs,
        compiler_params=_cparams(1),
        name="proj_odd",
    )(x, tr.mod, tr.mod, g_pre, w_in)


def _outproj_kernel(*refs, n_in, passes):
    a_refs = refs[:n_in]
    w_refs = refs[n_in:2 * n_in]
    x_ref, gate_ref, gp_ref, o_ref = refs[2 * n_in:]
    acc = _dot(_lhs(a_refs[0][...], passes), w_refs[0][...], passes)
    for a_ref, w_ref in zip(a_refs[1:], w_refs[1:]):
        acc = acc + _dot(_lhs(a_ref[...], passes), w_ref[...], passes)
    ms = jnp.mean(acc * acc, axis=-1, keepdims=True)
    y = acc * lax.rsqrt(ms + EPS) * gp_ref[...]
    o_ref[...] = x_ref[...] + gate_ref[...] * y


def _outproj(tr, acts, w, w_idx, x, li, gate_k, g_post, time_major=()):
    t = x.shape[0]
    tiles = max(tr.seq // tr.tm, 1)
    widths = [MIX_W if n in time_major else a.shape[1] for n, a in enumerate(acts)]
    assert all(wd == widths[0] for wd in widths)
    w_specs = [pl.BlockSpec((None, widths[0], D_MODEL), lambda i, n=n: (w_idx, n, 0)) for n in range(len(acts))]
    act_specs = [pl.BlockSpec((tr.tm, wd), lambda i: (i % tiles, i // tiles)) if n in time_major else tr.row_spec(wd)
                 for n, wd in enumerate(widths)]
    return pl.pallas_call(
        functools.partial(_outproj_kernel, n_in=len(acts), passes=tr.passes),
        out_shape=jax.ShapeDtypeStruct((t, D_MODEL), f32),
        grid=(t // tr.tm,),
        in_specs=act_specs + w_specs + [
            tr.row_spec(D_MODEL),
            tr.mod_spec(li, gate_k),
            _const_spec((None, 1, D_MODEL), (li, 0, 0)),
        ],
        out_specs=tr.row_spec(D_MODEL),
        compiler_params=_cparams(1),
        name="outproj",
    )(*acts, *([w] * len(acts)), x, tr.mod, g_post)


def _retention_kernel(q_ref, k_ref, v_ref, g_ref, intra_ref, qd_ref, kd_ref, gn_ref, o_ref, s_out_ref, s_scr, *, chunk_decay):
    c = pl.program_id(1)

    @pl.when(c == 0)
    def _():
        s_scr[...] = jnp.zeros_like(s_scr)

    heads = range(H_RET)
    sls = [slice(hh * DK_RET, (hh + 1) * DK_RET) for hh in heads]
    s_old = [s_scr[hh] for hh in heads]
    scores = [(_dot_nt(q_ref[:, sls[hh]], k_ref[:, sls[hh]]) * intra_ref[hh]).astype(bf16) for hh in heads]
    cross = [jnp.dot(q_ref[:, sls[hh]], s_old[hh].astype(bf16), preferred_element_type=f32) * qd_ref[hh] for hh in heads]
    kts = [(k_ref[:, sls[hh]].astype(f32) * kd_ref[hh]).T.astype(bf16) for hh in heads]
    outs = [jnp.dot(scores[hh], v_ref[:, sls[hh]], preferred_element_type=f32) + cross[hh] for hh in heads]
    for hh in heads:
        s_scr[hh] = s_old[hh] * chunk_decay[hh] + jnp.dot(kts[hh], v_ref[:, sls[hh]], preferred_element_type=f32)
    for hh in heads:
        sl = sls[hh]
        o = outs[hh]
        mu = jnp.mean(o, axis=-1, keepdims=True)
        d = o - mu
        var = jnp.mean(d * d, axis=-1, keepdims=True)
        on = d * lax.rsqrt(var + EPS) * gn_ref[hh]
        o_ref[:, sl] = (on * _silu(g_ref[:, sl])).astype(o_ref.dtype)

    @pl.when(c == pl.num_programs(1) - 1)
    def _():
        s_out_ref[...] = s_scr[...]


def _retention_tables(c):
    log_g = jnp.log1p(-jnp.exp2(-5.0 - jnp.arange(H_RET, dtype=f32)))
    idx = jnp.arange(c, dtype=f32)
    diff = idx[:, None] - idx[None, :]
    intra = jnp.where(diff >= 0, jnp.exp(jnp.maximum(diff, 0.0)[None] * log_g[:, None, None]), 0.0)
    q_decay = jnp.exp((idx + 1.0)[None, :] * log_g[:, None])[..., None]
    k_decay = jnp.exp((c - 1.0 - idx)[None, :] * log_g[:, None])[..., None]
    qd = jnp.broadcast_to(q_decay, (H_RET, c, DV_RET))
    kd = jnp.broadcast_to(k_decay, (H_RET, c, DK_RET))
    return intra, qd, kd


def _chunk_decay(c):
    return tuple(float(math.exp(c * math.log1p(-(2.0 ** (-5.0 - h))))) for h in range(H_RET))


def _retention(q, k, v, g, gn_g, bsz, seq):
    c = RET_CHUNK
    nc = seq // c
    intra, qd, kd = _retention_tables(c)
    row = pl.BlockSpec((c, MIX_W), lambda b, j: (b * nc + j, 0))
    tab = _const_spec((H_RET, c, c), (0, 0, 0))
    return pl.pallas_call(
        functools.partial(_retention_kernel, chunk_decay=_chunk_decay(c)),
        out_shape=[
            jax.ShapeDtypeStruct((bsz * seq, MIX_W), bf16),
            jax.ShapeDtypeStruct((bsz, H_RET, DK_RET, DV_RET), f32),
        ],
        grid=(bsz, nc),
        in_specs=[row, row, row, row, tab, tab, tab, _const_spec((H_RET, 1, DV_RET), (0, 0, 0))],
        out_specs=[row, pl.BlockSpec((None, H_RET, DK_RET, DV_RET), lambda b, j: (b, 0, 0, 0))],
        scratch_shapes=[pltpu.VMEM((H_RET, DK_RET, DV_RET), f32)],
        compiler_params=_cparams(2),
        name="retention",
    )(q, k, v, g, intra, qd, kd, gn_g.reshape(H_RET, 1, DV_RET))


U_GROUPS = S5_WIDTH // LANES
ST_PER_UG = S5_N // U_GROUPS


def _s5_params(lam_re, lam_im, log_step, b_re, b_im, c_re, c_im):
    dt = jnp.exp(log_step)[:, None]
    mag = jnp.exp(lam_re * dt)
    a_re, a_im = mag * jnp.cos(lam_im * dt), mag * jnp.sin(lam_im * dt)
    den = lam_re * lam_re + lam_im * lam_im
    coef_re = ((a_re - 1.0) * lam_re + a_im * lam_im) / den
    coef_im = (a_im * lam_re - (a_re - 1.0) * lam_im) / den
    bbar_re = coef_re[..., None] * b_re - coef_im[..., None] * b_im
    bbar_im = coef_re[..., None] * b_im + coef_im[..., None] * b_re
    gpb = LANES // S5_GROUP
    eye = jnp.eye(gpb, dtype=f32)

    def b_layout(bb):
        bb = bb.reshape(U_GROUPS, gpb, S5_STATE, S5_GROUP).transpose(0, 1, 3, 2)
        return jnp.einsum("jgip,gh->jgihp", bb, eye).reshape(U_GROUPS, LANES, ST_PER_UG)

    def c_layout(cc):
        cc = cc.reshape(U_GROUPS, gpb, S5_GROUP, S5_STATE)
        return jnp.einsum("jgip,gh->jgphi", cc, eye).reshape(U_GROUPS, ST_PER_UG, LANES)

    bw = jnp.concatenate([b_layout(bbar_re), b_layout(bbar_im)], axis=-1)
    cw = jnp.concatenate([c_layout(c_re), -c_layout(c_im)], axis=1)
    return a_re.reshape(1, S5_N), a_im.reshape(1, S5_N), bw, cw


S5_TS = 128
N_SLAB = S5_N // LANES
SLABS_PER_UG = ST_PER_UG // LANES


def _s5_kernel(u_ref, bw_ref, a_ref, cw_ref, d_ref, glu_ref, y_ref, xr_out, xi_out, xr_scr, xi_scr, car_scr, *, bsz):
    j = pl.program_id(0)

    @pl.when(j == 0)
    def _():
        car_scr[...] = jnp.zeros_like(car_scr)

    u = u_ref[...]
    ub = u.astype(bf16)
    for g in range(U_GROUPS):
        bu = jnp.dot(ub[:, g * LANES:(g + 1) * LANES], bw_ref[g], preferred_element_type=f32)
        for s in range(SLABS_PER_UG):
            xr_scr[g * SLABS_PER_UG + s] = bu[:, s * LANES:(s + 1) * LANES]
            xi_scr[g * SLABS_PER_UG + s] = bu[:, ST_PER_UG + s * LANES:ST_PER_UG + (s + 1) * LANES]

    def step(t, carry):
        at_t = pl.ds(pl.multiple_of(t * bsz, bsz), bsz)
        new = []
        for s in range(N_SLAB):
            xr, xi = carry[2 * s], carry[2 * s + 1]
            ar = a_ref[0, s]
            ai = a_ref[1, s]
            nr = (ar * xr - ai * xi) + xr_scr[s, at_t, :]
            ni = (ar * xi + ai * xr) + xi_scr[s, at_t, :]
            xr_scr[s, at_t, :] = nr
            xi_scr[s, at_t, :] = ni
            new += [nr, ni]
        return tuple(new)

    last = lax.fori_loop(0, S5_TS, step, tuple(car_scr[k] for k in range(2 * N_SLAB)))
    for k in range(2 * N_SLAB):
        car_scr[k] = last[k]

    ys = []
    for g in range(U_GROUPS):
        slabs = range(g * SLABS_PER_UG, (g + 1) * SLABS_PER_UG)
        xr_g = jnp.concatenate([xr_scr[s] for s in slabs], axis=1).astype(bf16)
        xi_g = jnp.concatenate([xi_scr[s] for s in slabs], axis=1).astype(bf16)
        ys.append(jnp.dot(xr_g, cw_ref[g, :ST_PER_UG, :], preferred_element_type=f32)
                  + jnp.dot(xi_g, cw_ref[g, ST_PER_UG:, :], preferred_element_type=f32))
    y = _gelu(jnp.concatenate(ys, axis=1) + d_ref[...] * u)
    z = jnp.dot(y.astype(bf16), glu_ref[...], preferred_element_type=f32)
    y_ref[...] = (y * _sigmoid(z)).astype(y_ref.dtype)

    @pl.when(j == pl.num_programs(0) - 1)
    def _():
        for s in range(N_SLAB):
            xr_out[:, s * LANES:(s + 1) * LANES] = last[2 * s]
            xi_out[:, s * LANES:(s + 1) * LANES] = last[2 * s + 1]


def _s5(u_tb, bw, a_re, a_im, cw, d_skip, w_glu, bsz, seq):
    rows = S5_TS * bsz
    a_tab = jnp.stack([a_re, a_im]).reshape(2, N_SLAB, 1, LANES)
    a_tab = jnp.broadcast_to(a_tab, (2, N_SLAB, bsz, LANES))
    st = _const_spec((bsz, S5_N), (0, 0))
    row = pl.BlockSpec((rows, S5_WIDTH), lambda j: (j, 0))
    y, xr, xi = pl.pallas_call(
        functools.partial(_s5_kernel, bsz=bsz),
        out_shape=[
            jax.ShapeDtypeStruct((seq * bsz, S5_WIDTH), bf16),
            jax.ShapeDtypeStruct((bsz, S5_N), f32),
            jax.ShapeDtypeStruct((bsz, S5_N), f32),
        ],
        grid=(seq // S5_TS,),
        in_specs=[
            row,
            _const_spec((U_GROUPS, LANES, 2 * ST_PER_UG), (0, 0, 0)),
            _const_spec((2, N_SLAB, bsz, LANES), (0, 0, 0, 0)),
            _const_spec((U_GROUPS, 2 * ST_PER_UG, LANES), (0, 0, 0)),
            _const_spec((1, S5_WIDTH), (0, 0)),
            _const_spec((S5_WIDTH, S5_WIDTH), (0, 0)),
        ],
        out_specs=[row, st, st],
        scratch_shapes=[
            pltpu.VMEM((N_SLAB, rows, LANES), f32),
            pltpu.VMEM((N_SLAB, rows, LANES), f32),
            pltpu.VMEM((2 * N_SLAB, bsz, LANES), f32),
        ],
        compiler_params=_cparams(1),
        name="s5",
    )(u_tb.reshape(seq * bsz, S5_WIDTH), bw, a_tab, cw, d_skip, w_glu)
    return y.reshape(seq, bsz * S5_WIDTH), xr.reshape(bsz, S5_GROUPS, S5_STATE), xi.reshape(bsz, S5_GROUPS, S5_STATE)


FF_CHUNK = 256
FFN_HALO = SUBLANES


def _ffn_up_kernel(x_ref, xh_ref, sc_ref, sh_ref, g_ref, w_ref, cw_ref, cb_ref, act_ref, cs_ref, ext_scr, *, tiles_per_seq):
    i = pl.program_id(0)
    tm = x_ref.shape[0]
    g = g_ref[...]
    sc = sc_ref[...]
    sh = sh_ref[...]
    h = _norm_mod(x_ref[...], g, sc, sh).astype(bf16)
    hh = _norm_mod(xh_ref[...], g, sc, sh).astype(bf16)
    keep = (i % tiles_per_seq != 0).astype(f32)
    for n in range(D_FF // FF_CHUNK):
        mixed = []
        for half in range(2):
            cs = slice(half * D_FF + n * FF_CHUNK, half * D_FF + (n + 1) * FF_CHUNK)
            w = w_ref[:, cs]
            up = jnp.dot(h, w, preferred_element_type=f32)
            halo = jnp.dot(hh, w, preferred_element_type=f32) * keep
            ext_scr[half, 0:FFN_HALO, :] = halo
            ext_scr[half, FFN_HALO:, :] = up
            r1 = ext_scr[half, pl.ds(FFN_HALO - 1, tm), :]
            r2 = ext_scr[half, pl.ds(FFN_HALO - 2, tm), :]
            mixed.append(cw_ref[0:1, cs] * r2 + cw_ref[1:2, cs] * r1 + cw_ref[2:3, cs] * up + cb_ref[:, cs])
            cs_ref[:, cs] = up[tm - (CONV_W - 1):, :]
        act_ref[:, n * FF_CHUNK:(n + 1) * FF_CHUNK] = (mixed[0] * _gelu(mixed[1])).astype(act_ref.dtype)


def _ffn_up(tr, x, li, g_pre, w_up, conv_w, conv_b, bsz):
    t = x.shape[0]
    tm = tr.tm
    tiles = tr.seq // tm
    halo_blocks = tm // FFN_HALO
    return pl.pallas_call(
        functools.partial(_ffn_up_kernel, tiles_per_seq=tiles),
        out_shape=[
            jax.ShapeDtypeStruct((t, D_FF), bf16),
            jax.ShapeDtypeStruct((bsz, CONV_W - 1, 2 * D_FF), f32),
        ],
        grid=(t // tm,),
        in_specs=[
            tr.row_spec(D_MODEL),
            pl.BlockSpec((FFN_HALO, D_MODEL), lambda i: (jnp.maximum(i * halo_blocks - 1, 0), 0)),
            tr.mod_spec(li, 4),
            tr.mod_spec(li, 3),
            _const_spec((None, 1, D_MODEL), (li, 0, 0)),
            _const_spec((None, D_MODEL, 2 * D_FF), (li, 0, 0)),
            _const_spec((None, CONV_W, 2 * D_FF), (li, 0, 0)),
            _const_spec((None, 1, 2 * D_FF), (li, 0, 0)),
        ],
        out_specs=[
            tr.row_spec(D_FF),
            pl.BlockSpec((None, CONV_W - 1, 2 * D_FF), lambda i: (i // tiles, 0, 0)),
        ],
        scratch_shapes=[pltpu.VMEM((2, tm + FFN_HALO, FF_CHUNK), f32)],
        compiler_params=_cparams(1),
        name="ffn_up",
    )(x, x, tr.mod, tr.mod, g_pre, w_up, conv_w, conv_b)


def _t5_bucket(dist):
    n = jnp.maximum(dist, 0)
    max_exact = N_BUCKETS // 2
    large = max_exact + (jnp.log(jnp.maximum(n, 1).astype(f32) / max_exact)
                         / math.log(MAX_DISTANCE / max_exact) * (N_BUCKETS - max_exact)).astype(jnp.int32)
    large = jnp.minimum(large, N_BUCKETS - 1)
    return jnp.where(n < max_exact, n, large)


def _bias_by_distance(rel_bias, n):
    return rel_bias[_t5_bucket(jnp.arange(n))]


ATT_T = 512
SUFFIX_W = 256
SB_HEAD_GROUP = 4


def _diff_tile(q_ref, k_ref, v_ref, bias_ref, m_scr, l_scr, acc_scr, diagonal):
    t = q_ref.shape[0]
    if diagonal:
        causal = lax.broadcasted_iota(jnp.int32, (t, t), 0) >= lax.broadcasted_iota(jnp.int32, (t, t), 1)
    streams = range(2 * H_DIFF)
    scores = []
    for n in streams:
        sl = slice(n * DH_DIFF, (n + 1) * DH_DIFF)
        s = _dot_nt(q_ref[:, sl], k_ref[:, sl]) + bias_ref[n // 2]
        scores.append(jnp.where(causal, s, NEG_BIG) if diagonal else s)
    probs, alphas = [], []
    for n in streams:
        m_old = m_scr[n]
        m_new = jnp.maximum(m_old, jnp.max(scores[n], axis=-1, keepdims=True))
        p = jnp.exp(scores[n] - m_new)
        alpha = jnp.exp(m_old - m_new)
        l_scr[n] = alpha * l_scr[n] + jnp.sum(p, axis=-1, keepdims=True)
        m_scr[n] = m_new
        probs.append(p.astype(bf16))
        alphas.append(alpha)
    for n in streams:
        vh = v_ref[:, (n // 2) * 2 * DH_DIFF:(n // 2 + 1) * 2 * DH_DIFF]
        acc_scr[n] = alphas[n] * acc_scr[n] + jnp.dot(probs[n], vh, preferred_element_type=f32)


def _diff_attn_kernel(lam_ref, q_ref, k_ref, v_ref, bias_ref, sg_ref, o_ref, m_scr, l_scr, acc_scr, *, out_scale):
    qi = pl.program_id(1)
    kj = pl.program_id(2)

    @pl.when(kj == 0)
    def _():
        m_scr[...] = jnp.full_like(m_scr, NEG_BIG)
        l_scr[...] = jnp.zeros_like(l_scr)
        acc_scr[...] = jnp.zeros_like(acc_scr)

    @pl.when(kj < qi)
    def _():
        _diff_tile(q_ref, k_ref, v_ref, bias_ref, m_scr, l_scr, acc_scr, False)

    @pl.when(kj == qi)
    def _():
        _diff_tile(q_ref, k_ref, v_ref, bias_ref, m_scr, l_scr, acc_scr, True)
        lam = lam_ref[0]
        for hh in range(H_DIFF):
            o = acc_scr[2 * hh] / l_scr[2 * hh] - lam * (acc_scr[2 * hh + 1] / l_scr[2 * hh + 1])
            ms = jnp.mean(o * o, axis=-1, keepdims=True)
            o = o * lax.rsqrt(ms + EPS) * sg_ref[...] * out_scale
            o_ref[:, hh * 2 * DH_DIFF:(hh + 1) * 2 * DH_DIFF] = o.astype(o_ref.dtype)


def _diff_attn(q, k, v, bias_tiles, lam, subln_g, lam_init, bsz, seq):
    t = ATT_T
    nq = seq // t
    qspec = pl.BlockSpec((t, MIX_W), lambda b, i, j: (b * nq + i, 0))
    kspec = pl.BlockSpec((t, MIX_W), lambda b, i, j: (b * nq + jnp.minimum(j, i), 0))
    return pl.pallas_call(
        functools.partial(_diff_attn_kernel, out_scale=1.0 - lam_init),
        out_shape=jax.ShapeDtypeStruct((bsz * seq, MIX_W), bf16),
        grid=(bsz, nq, nq),
        in_specs=[
            pl.BlockSpec(memory_space=pltpu.SMEM),
            qspec,
            kspec,
            kspec,
            pl.BlockSpec((H_DIFF, None, t, t), lambda b, i, j: (0, jnp.maximum(i - j, 0), 0, 0)),
            _const_spec((1, 2 * DH_DIFF), (0, 0)),
        ],
        out_specs=qspec,
        scratch_shapes=[
            pltpu.VMEM((2 * H_DIFF, t, 1), f32),
            pltpu.VMEM((2 * H_DIFF, t, 1), f32),
            pltpu.VMEM((2 * H_DIFF, t, 2 * DH_DIFF), f32),
        ],
        compiler_params=_cparams(3),
        name="diff_attn",
    )(lam, q, k, v, bias_tiles, subln_g)


def _log_sigmoid_pair(z):
    ls = jnp.minimum(z, 0.0) - jnp.log(1.0 + jnp.exp(-jnp.abs(z)))
    return ls, ls - z


def _suffix_sum(lk, upper):
    hi, lo = _split(lk)
    return jnp.dot(hi, upper, preferred_element_type=f32) + jnp.dot(lo, upper, preferred_element_type=f32)


def _upper_ones(t):
    j = lax.broadcasted_iota(jnp.int32, (t, t), 0)
    k = lax.broadcasted_iota(jnp.int32, (t, t), 1)
    return jnp.where(j > k, 1.0, 0.0).astype(bf16)


def _sb_attn_kernel(q_ref, k_ref, v_ref, o_ref, r_scr, acc_scr):
    qi = pl.program_id(1)
    st = pl.program_id(2)
    t = q_ref.shape[0]

    @pl.when(st == 0)
    def _():
        r_scr[...] = jnp.zeros_like(r_scr)
        acc_scr[...] = jnp.zeros_like(acc_scr)

    def tile(diagonal):
        if diagonal:
            strict = lax.broadcasted_iota(jnp.int32, (t, t), 1) < lax.broadcasted_iota(jnp.int32, (t, t), 0)
        upper = _upper_ones(SUFFIX_W)
        nblk = t // SUFFIX_W
        for g0 in range(0, H_SB, SB_HEAD_GROUP):
            heads = range(g0, g0 + SB_HEAD_GROUP)
            pairs = {}
            for hh in heads:
                sl = slice(hh * DH_SB, (hh + 1) * DH_SB)
                ls, lk = _log_sigmoid_pair(_dot_nt(q_ref[:, sl], k_ref[:, sl]))
                pairs[hh] = (ls, jnp.where(strict, lk, 0.0) if diagonal else lk)
            sufs = {hh: [_suffix_sum(pairs[hh][1][:, c * SUFFIX_W:(c + 1) * SUFFIX_W], upper) for c in range(nblk)]
                    for hh in heads}
            ws = {}
            for hh in heads:
                ls, lk = pairs[hh]
                carry = r_scr[hh]
                afters = [None] * nblk
                for c in reversed(range(nblk)):
                    afters[c] = sufs[hh][c] + carry
                    carry = carry + jnp.sum(lk[:, c * SUFFIX_W:(c + 1) * SUFFIX_W], axis=-1, keepdims=True)
                r_scr[hh] = carry
                w = jnp.exp(ls + jnp.concatenate(afters, axis=1))
                ws[hh] = (jnp.where(strict, w, 0.0) if diagonal else w).astype(bf16)
            for hh in heads:
                sl = slice(hh * DH_SB, (hh + 1) * DH_SB)
                acc_scr[hh] = acc_scr[hh] + jnp.dot(ws[hh], v_ref[:, sl], preferred_element_type=f32)

    @pl.when(st == 0)
    def _():
        tile(True)

    @pl.when(jnp.logical_and(st > 0, st <= qi))
    def _():
        tile(False)

    @pl.when(st == qi)
    def _():
        for hh in range(H_SB):
            o_ref[:, hh * DH_SB:(hh + 1) * DH_SB] = acc_scr[hh].astype(o_ref.dtype)


def _sb_attn(q, k, v, bsz, seq):
    t = ATT_T
    nq = seq // t
    qspec = pl.BlockSpec((t, MIX_W), lambda b, i, s: (b * nq + i, 0))
    kspec = pl.BlockSpec((t, MIX_W), lambda b, i, s: (b * nq + jnp.maximum(i - s, 0), 0))
    return pl.pallas_call(
        _sb_attn_kernel,
        out_shape=jax.ShapeDtypeStruct((bsz * seq, MIX_W), bf16),
        grid=(bsz, nq, nq),
        in_specs=[qspec, kspec, kspec],
        out_specs=qspec,
        scratch_shapes=[pltpu.VMEM((H_SB, t, 1), f32), pltpu.VMEM((H_SB, t, DH_SB), f32)],
        compiler_params=_cparams(3),
        name="sb_attn",
    )(q, k, v)


DEC_BB = 8


def _even_dec_kernel(q_ref, k_ref, v_ref, g_ref, u_ref, s0_ref, x0r_ref, x0i_ref, gn_ref, bw_ref, are_ref, aim_ref,
                     cw_ref, d_ref, glu_ref, o_ref, y_ref, s_ref, xr_ref, xi_ref, *, decay):
    pad = jnp.zeros((DK_RET - DEC_BB, DK_RET), f32)
    for hh in range(H_RET):
        sl = slice(hh * DK_RET, (hh + 1) * DK_RET)
        qh = q_ref[:, sl]
        kh = k_ref[:, sl]
        vh = v_ref[:, sl]
        q_t = jnp.concatenate([qh, pad], axis=0).T
        k_t = jnp.concatenate([kh, pad], axis=0).T
        qk = jnp.sum(qh * kh, axis=-1, keepdims=True)
        rows = []
        for b in range(DEC_BB):
            s0 = s0_ref[b, hh]
            qs = jnp.sum(q_t[:, b:b + 1] * s0, axis=0, keepdims=True)
            rows.append(qk[b:b + 1] * vh[b:b + 1] + qs * decay[hh])
            s_ref[b, hh] = s0 * decay[hh] + k_t[:, b:b + 1] * vh[b:b + 1]
        o = jnp.concatenate(rows, axis=0)
        mu = jnp.mean(o, axis=-1, keepdims=True)
        dlt = o - mu
        var = jnp.mean(dlt * dlt, axis=-1, keepdims=True)
        on = dlt * lax.rsqrt(var + EPS) * gn_ref[hh]
        o_ref[:, sl] = on * _silu(g_ref[:, sl])

    u = u_ref[...]
    bus = [_dot(_lhs(u[:, j * LANES:(j + 1) * LANES], 3), bw_ref[j], 3) for j in range(U_GROUPS)]
    bu_re = jnp.concatenate([b[:, :ST_PER_UG] for b in bus], axis=1)
    bu_im = jnp.concatenate([b[:, ST_PER_UG:] for b in bus], axis=1)
    are = are_ref[...]
    aim = aim_ref[...]
    x0r = x0r_ref[...]
    x0i = x0i_ref[...]
    xr = (are * x0r - aim * x0i) + bu_re
    xi = (are * x0i + aim * x0r) + bu_im
    xr_ref[...] = xr
    xi_ref[...] = xi
    ys = []
    for j in range(U_GROUPS):
        ss = slice(j * ST_PER_UG, (j + 1) * ST_PER_UG)
        ys.append(_dot(_lhs(xr[:, ss], 3), cw_ref[j, :ST_PER_UG, :], 3) + _dot(_lhs(xi[:, ss], 3), cw_ref[j, ST_PER_UG:, :], 3))
    y = _gelu(jnp.concatenate(ys, axis=1) + d_ref[...] * u)
    z = _dot(_lhs(y, 3), glu_ref[...], 3)
    y_ref[...] = y * _sigmoid(z)


def _even_dec(q, k, v, g, u, state_ret, s5_re, s5_im, ie, gn_g, bw, a_re, a_im, cw, d_skip, w_glu):
    nb = q.shape[0]
    row = pl.BlockSpec((DEC_BB, MIX_W), lambda i: (i, 0))
    st_spec = pl.BlockSpec((None, DEC_BB, H_RET, DK_RET, DV_RET), lambda i: (ie, i, 0, 0, 0))
    x_spec = pl.BlockSpec((None, DEC_BB, S5_N), lambda i: (ie, i, 0))
    decay = tuple(float(math.exp(math.log1p(-(2.0 ** (-5.0 - h))))) for h in range(H_RET))
    return pl.pallas_call(
        functools.partial(_even_dec_kernel, decay=decay),
        out_shape=[
            jax.ShapeDtypeStruct((nb, MIX_W), f32),
            jax.ShapeDtypeStruct((nb, S5_WIDTH), f32),
            jax.ShapeDtypeStruct((nb, H_RET, DK_RET, DV_RET), f32),
            jax.ShapeDtypeStruct((nb, S5_N), f32),
            jax.ShapeDtypeStruct((nb, S5_N), f32),
        ],
        grid=(nb // DEC_BB,),
        in_specs=[
            row, row, row, row, row, st_spec, x_spec, x_spec,
            _const_spec((H_RET, 1, DV_RET), (0, 0, 0)),
            _const_spec((U_GROUPS, LANES, 2 * ST_PER_UG), (0, 0, 0)),
            _const_spec((1, S5_N), (0, 0)),
            _const_spec((1, S5_N), (0, 0)),
            _const_spec((U_GROUPS, 2 * ST_PER_UG, LANES), (0, 0, 0)),
            _const_spec((1, S5_WIDTH), (0, 0)),
            _const_spec((S5_WIDTH, S5_WIDTH), (0, 0)),
        ],
        out_specs=[
            row, row,
            pl.BlockSpec((DEC_BB, H_RET, DK_RET, DV_RET), lambda i: (i, 0, 0, 0)),
            pl.BlockSpec((DEC_BB, S5_N), lambda i: (i, 0)),
            pl.BlockSpec((DEC_BB, S5_N), lambda i: (i, 0)),
        ],
        compiler_params=_cparams(1),
        name="even_dec",
    )(q, k, v, g, u, state_ret, s5_re, s5_im, gn_g.reshape(H_RET, 1, DV_RET), bw, a_re, a_im, cw, d_skip, w_glu)


def _ffn_up_dec_kernel(x_ref, sc_ref, sh_ref, g_ref, wv_ref, wg_ref, b0v_ref, b0g_ref, b1v_ref, b1g_ref,
                       cwv_ref, cwg_ref, cbv_ref, cbg_ref, act_ref, upv_ref, upg_ref):
    h = _lhs(_norm_mod(x_ref[...], g_ref[...], sc_ref[...], sh_ref[...]), 3)
    upv = _dot(h, wv_ref[...], 3)
    upg = _dot(h, wg_ref[...], 3)
    val = cwv_ref[0:1, :] * b0v_ref[...] + cwv_ref[1:2, :] * b1v_ref[...] + cwv_ref[2:3, :] * upv + cbv_ref[...]
    gate = cwg_ref[0:1, :] * b0g_ref[...] + cwg_ref[1:2, :] * b1g_ref[...] + cwg_ref[2:3, :] * upg + cbg_ref[...]
    act_ref[...] = val * _gelu(gate)
    upv_ref[...] = upv
    upg_ref[...] = upg


def _ffn_up_dec(x, mod_s, li, g_pre, w_up, conv_w, conv_b, state_conv2d):
    nb = x.shape[0]
    nch = D_FF // FF_CHUNK
    full = _const_spec((nb, D_MODEL), (0, 0))

    def mod(k):
        return pl.BlockSpec((None, nb, D_MODEL), lambda n: (li, 0, k))

    def cols(rows, off):
        return pl.BlockSpec((None, rows, FF_CHUNK), lambda n: (li, 0, n + off))

    out = pl.BlockSpec((nb, FF_CHUNK), lambda n: (0, n))
    return pl.pallas_call(
        _ffn_up_dec_kernel,
        out_shape=[jax.ShapeDtypeStruct((nb, D_FF), f32)] * 3,
        grid=(nch,),
        in_specs=[
            full, mod(4), mod(3), _const_spec((None, 1, D_MODEL), (li, 0, 0)),
            cols(D_MODEL, 0), cols(D_MODEL, nch),
            cols(nb, 0), cols(nb, nch), cols(nb, 2 * nch), cols(nb, 3 * nch),
            cols(CONV_W, 0), cols(CONV_W, nch), cols(1, 0), cols(1, nch),
        ],
        out_specs=[out, out, out],
        compiler_params=_cparams(1),
        name="ffn_up_dec",
    )(x, mod_s, mod_s, g_pre, w_up, w_up, state_conv2d, state_conv2d, state_conv2d, state_conv2d,
      conv_w, conv_w, conv_b, conv_b)


DEC_PP = 8


def _attn_dec_kernel(pt_ref, lam_ref, qd_ref, qs_ref, kn_ref, vn_ref, bias_ref, bias0_ref, sg_ref, *rest, n_pages, out_scale):
    del pt_ref
    pages = rest[:4 * DEC_PP]
    o_ref, md, ld, accd, rs, accs = rest[4 * DEC_PP:]
    j = pl.program_id(1)
    rows = H_SB
    hd = 2 * DH_DIFF
    head_of_lane = lax.broadcasted_iota(jnp.int32, (rows, MIX_W), 1) // DH_SB
    q_sb = jnp.where(head_of_lane == lax.broadcasted_iota(jnp.int32, (rows, MIX_W), 0), qs_ref[...], 0.0)

    def per_row_head(ref):
        return jnp.concatenate([ref[:, (r // 2) * hd:(r // 2 + 1) * hd] for r in range(rows)], axis=0)

    half_of_lane = lax.broadcasted_iota(jnp.int32, (rows, hd), 1) // DH_DIFF
    row_id = lax.broadcasted_iota(jnp.int32, (rows, hd), 0)
    q_diff = jnp.where(half_of_lane == row_id % 2, per_row_head(qd_ref), 0.0)

    @pl.when(j == 0)
    def _():
        md[...] = jnp.sum(q_diff * per_row_head(kn_ref), axis=-1, keepdims=True) + bias0_ref[...]
        ld[...] = jnp.ones_like(ld)
        accd[...] = per_row_head(vn_ref)
        rs[...] = jnp.zeros_like(rs)
        accs[...] = jnp.zeros_like(accs)

    qd_b = q_diff.astype(bf16)
    qs_b = q_sb.astype(bf16)
    upper = _upper_ones(PAGE_SIZE)

    flat = PAGE_SIZE * H_DIFF
    own = lax.broadcasted_iota(jnp.int32, (rows, flat), 1) % H_DIFF == lax.broadcasted_iota(jnp.int32, (rows, flat), 0) // 2
    scores = []
    for i in range(DEC_PP):
        page = n_pages - 1 - (j * DEC_PP + i)
        s = _dot_nt(qd_b, pages[4 * i][...].astype(bf16)) + bias_ref[page]
        scores.append(jnp.where(own, s, NEG_BIG))
    m_old = md[...]
    m_new = m_old
    for s in scores:
        m_new = jnp.maximum(m_new, jnp.max(s, axis=-1, keepdims=True))
    alpha = jnp.exp(m_old - m_new)
    l_new = alpha * ld[...]
    acc_d = alpha * accd[...]
    for i, s in enumerate(scores):
        p = jnp.exp(s - m_new)
        l_new = l_new + jnp.sum(p, axis=-1, keepdims=True)
        acc_d = acc_d + jnp.dot(p.astype(bf16), pages[4 * i + 1][...].astype(bf16), preferred_element_type=f32)
    ld[...] = l_new
    md[...] = m_new
    accd[...] = acc_d

    zs = [jnp.dot(qs_b, pages[4 * i + 2][...].reshape(MIX_W, PAGE_SIZE).astype(bf16), preferred_element_type=f32)
          for i in range(DEC_PP)]
    pairs = [_log_sigmoid_pair(z) for z in zs]
    sufs = [_suffix_sum(lk, upper) for _, lk in pairs]
    carry = rs[...]
    ws = []
    for (ls, lk), suf in zip(pairs, sufs):
        ws.append(jnp.exp(ls + (suf + carry)).astype(bf16))
        carry = carry + jnp.sum(lk, axis=-1, keepdims=True)
    acc_s = accs[...]
    for i, w in enumerate(ws):
        acc_s = acc_s + _dot_nt(w, pages[4 * i + 3][...].reshape(MIX_W, PAGE_SIZE).astype(bf16))
    rs[...] = carry
    accs[...] = acc_s

    @pl.when(j == pl.num_programs(1) - 1)
    def _():
        lam = lam_ref[0]
        acc = accd[...] / ld[...]
        pieces = []
        for hh in range(H_DIFF):
            o = acc[2 * hh:2 * hh + 1] - lam * acc[2 * hh + 1:2 * hh + 2]
            ms = jnp.mean(o * o, axis=-1, keepdims=True)
            pieces.append(o * lax.rsqrt(ms + EPS) * sg_ref[...] * out_scale)
        a_sb = accs[...]
        for hh in range(H_SB):
            pieces.append(a_sb[hh:hh + 1, hh * DH_SB:(hh + 1) * DH_SB])
        o_ref[...] = jnp.concatenate(pieces, axis=1)


def _attn_dec(page_table, lam, qd, qs, k_new, v_new, bias_pages, bias0, subln_g, caches, io, lam_init):
    nb, n_pages = page_table.shape
    row3 = lambda a: a.reshape(nb, 1, a.shape[-1])
    rspec = pl.BlockSpec((None, 1, MIX_W), lambda b, j, pt: (b, 0, 0))

    def page_spec(i, cache):
        blk = (None, None) + cache.shape[2:]
        zeros = (0,) * (cache.ndim - 2)
        return pl.BlockSpec(blk, lambda b, j, pt: (io, pt[b, n_pages - 1 - (j * DEC_PP + i)]) + zeros)

    page_specs, page_args = [], []
    for i in range(DEC_PP):
        for cache in caches:
            page_specs.append(page_spec(i, cache))
            page_args.append(cache)
    grid_spec = pltpu.PrefetchScalarGridSpec(
        num_scalar_prefetch=1,
        grid=(nb, n_pages // DEC_PP),
        in_specs=[
            pl.BlockSpec(memory_space=pltpu.SMEM),
            rspec, rspec, rspec, rspec,
            pl.BlockSpec((n_pages, 1, PAGE_SIZE * H_DIFF), lambda b, j, pt: (0, 0, 0)),
            pl.BlockSpec((H_SB, 1), lambda b, j, pt: (0, 0)),
            pl.BlockSpec((1, 2 * DH_DIFF), lambda b, j, pt: (0, 0)),
        ] + page_specs,
        out_specs=pl.BlockSpec((None, 1, 2 * MIX_W), lambda b, j, pt: (b, 0, 0)),
        scratch_shapes=[
            pltpu.VMEM((H_SB, 1), f32), pltpu.VMEM((H_SB, 1), f32), pltpu.VMEM((H_SB, 2 * DH_DIFF), f32),
            pltpu.VMEM((H_SB, 1), f32), pltpu.VMEM((H_SB, MIX_W), f32),
        ],
    )
    out = pl.pallas_call(
        functools.partial(_attn_dec_kernel, n_pages=n_pages, out_scale=1.0 - lam_init),
        out_shape=jax.ShapeDtypeStruct((nb, 1, 2 * MIX_W), f32),
        grid_spec=grid_spec,
        compiler_params=_cparams(2),
        name="attn_dec",
    )(page_table, lam, row3(qd), row3(qs), row3(k_new), row3(v_new), bias_pages, bias0, subln_g, *page_args)
    return out.reshape(nb, 2 * MIX_W)


def _lam_init(li):
    return 0.8 - 0.6 * math.exp(-0.3 * li)


def _diff_lambda(lam_vecs, li):
    lv = lam_vecs.astype(f32)
    lam = jnp.exp(jnp.dot(lv[0], lv[1])) - jnp.exp(jnp.dot(lv[2], lv[3])) + _lam_init(li)
    return lam.reshape(1)


def _rope_tables(pos):
    half = DK_RET // 2
    inv = jnp.power(ROPE_BASE, -jnp.arange(half, dtype=f32) / half)
    ang = pos.astype(f32)[:, None] * inv[None, :]
    cos, sin = jnp.cos(ang), jnp.sin(ang)
    return jnp.concatenate([cos, cos], axis=-1), jnp.concatenate([-sin, sin], axis=-1)


def _prompt_bias_tiles(bias_d, seq):
    t = ATT_T
    nd = seq // t
    m = jnp.arange(2 * t)
    r_minus_c = jnp.where(m < t, -m, 2 * t - m)
    dist = jnp.arange(nd)[:, None] * t + r_minus_c[None, :]
    gen = jnp.transpose(bias_d[jnp.clip(dist, 0, seq - 1)], (2, 0, 1))
    rows = jnp.tile(gen, (1, 1, t))[:, :, :t * (2 * t - 1)].reshape(gen.shape[0], nd, t, 2 * t - 1)
    return rows[:, :, :, :t]


def _prompt_trunk(x_prompt, mod_p, prm):
    bsz, seq, _ = x_prompt.shape
    tr = _Trunk(mod_p, seq, min(1024, seq), 1)
    tr_ffn = _Trunk(mod_p, seq, min(512, seq), 1)
    x = x_prompt.reshape(bsz * seq, D_MODEL)
    cos2, sin2 = _rope_tables(jnp.arange(seq))
    bias_tiles = _prompt_bias_tiles(_bias_by_distance(prm["rel_bias"], seq), seq)
    new = {k: [] for k in ("ret", "s5r", "s5i", "kd", "vd", "ks", "vs", "conv")}
    for li in range(DEPTH):
        if li % 2 == 0:
            ie = li // 2
            s5p = prm["s5"][ie]
            q, k, v, g, u = _proj_even(tr, x, li, prm["g_pre_mix"], prm["w_in_even_b"], cos2, sin2, (bf16, bf16, bf16, f32, f32))
            o, ret_s = _retention(q, k, v, g, prm["ret_gn_g"][ie], bsz, seq)
            y, xr, xi = _s5(u, s5p["bw"].astype(bf16), s5p["a_re"], s5p["a_im"], s5p["cw"].astype(bf16), s5p["d"],
                            s5p["glu_b"], bsz, seq)
            new["ret"].append(ret_s)
            new["s5r"].append(xr)
            new["s5i"].append(xi)
            x = _outproj(tr, [o, y], prm["w_out_even_b"], ie, x, li, 2, prm["g_post_mix"], time_major=(1,))
        else:
            io = li // 2
            dq, dk, dv, sq, sk, sv, dk_b, dv_b, sk_b, sv_b = _proj_odd(tr, x, li, prm["g_pre_mix"], prm["w_in_odd_b"], True)
            lam = _diff_lambda(prm["diff_lam"][io], li)
            od = _diff_attn(dq, dk_b, dv_b, bias_tiles, lam, prm["diff_subln_g"][io].reshape(1, -1), _lam_init(li), bsz, seq)
            osb = _sb_attn(sq, sk_b, sv_b, bsz, seq)
            new["kd"].append(dk.reshape(bsz, seq, H_DIFF, 2 * DH_DIFF))
            new["vd"].append(dv.reshape(bsz, seq, H_DIFF, 2 * DH_DIFF))
            new["ks"].append(sk.reshape(bsz, seq, H_SB, DH_SB))
            new["vs"].append(sv.reshape(bsz, seq, H_SB, DH_SB))
            x = _outproj(tr, [od, osb], prm["w_out_odd_b"], io, x, li, 2, prm["g_post_mix"])
        act, cs = _ffn_up(tr_ffn, x, li, prm["g_pre_ffn"], prm["w_up_b"], prm["conv_w"], prm["conv_b"], bsz)
        new["conv"].append(cs)
        x = _outproj(tr, [act], prm["w_down_b"], li, x, li, 5, prm["g_post_ffn"])
    st = lambda t: jnp.stack(t, axis=0)
    return x.reshape(bsz, seq, D_MODEL), tuple(st(new[k]) for k in ("ret", "s5r", "s5i", "kd", "vd", "ks", "vs", "conv"))


def _sample_trunk(x_sample, mod_s, prm, past):
    nb = x_sample.shape[0]
    tr = _Trunk(mod_s, 1, nb, 3)
    x = x_sample.reshape(nb, D_MODEL)
    page_table = past["page_table"]
    n_pages = page_table.shape[1]
    q_off = n_pages * PAGE_SIZE
    cos2, sin2 = _rope_tables(jnp.full((1,), q_off))
    bias_d = _bias_by_distance(prm["rel_bias"], q_off + 1)
    key_pos = jnp.arange(q_off).reshape(n_pages, PAGE_SIZE)
    bias_pages = bias_d[q_off - key_pos].reshape(n_pages, 1, PAGE_SIZE * H_DIFF)
    bias0 = jnp.repeat(bias_d[0], 2).reshape(2 * H_DIFF, 1)
    s5_re = past["s5_re"].reshape(past["s5_re"].shape[0], nb, S5_N)
    s5_im = past["s5_im"].reshape(past["s5_im"].shape[0], nb, S5_N)
    conv2d = past["conv"].reshape(DEPTH, nb, (CONV_W - 1) * 2 * D_FF)
    caches = tuple(past[k].reshape(past[k].shape[0], past[k].shape[1], PAGE_SIZE * H_DIFF, 2 * DH_DIFF)
                   for k in ("diff_k", "diff_v"))
    caches += tuple(jnp.transpose(past[k], (0, 1, 3, 4, 2)) for k in ("sb_k", "sb_v"))
    new = {k: [] for k in ("ret", "s5r", "s5i", "kd", "vd", "ks", "vs", "conv")}
    for li in range(DEPTH):
        if li % 2 == 0:
            ie = li // 2
            s5p = prm["s5"][ie]
            q, k, v, g, u = _proj_even(tr, x, li, prm["g_pre_mix"], prm["w_in_even"], cos2, sin2, (f32,) * 5)
            o, y, ret_s, xr, xi = _even_dec(q, k, v, g, u, past["ret"], s5_re, s5_im, ie, prm["ret_gn_g"][ie],
                                            s5p["bw"], s5p["a_re"], s5p["a_im"], s5p["cw"], s5p["d"], s5p["glu"])
            new["ret"].append(ret_s)
            new["s5r"].append(xr.reshape(nb, S5_GROUPS, S5_STATE))
            new["s5i"].append(xi.reshape(nb, S5_GROUPS, S5_STATE))
            x = _outproj(tr, [o, y], prm["w_out_even"], ie, x, li, 2, prm["g_post_mix"])
        else:
            io = li // 2
            dq, dk, dv, sq, sk, sv = _proj_odd(tr, x, li, prm["g_pre_mix"], prm["w_in_odd"], False)
            lam = _diff_lambda(prm["diff_lam"][io], li)
            mix = _attn_dec(page_table, lam, dq, sq, dk, dv, bias_pages, bias0, prm["diff_subln_g"][io].reshape(1, -1),
                            caches, io, _lam_init(li))
            new["kd"].append(dk.reshape(nb, 1, H_DIFF, 2 * DH_DIFF))
            new["vd"].append(dv.reshape(nb, 1, H_DIFF, 2 * DH_DIFF))
            new["ks"].append(sk.reshape(nb, 1, H_SB, DH_SB))
            new["vs"].append(sv.reshape(nb, 1, H_SB, DH_SB))
            x = _outproj(tr, [mix[:, :MIX_W], mix[:, MIX_W:]], prm["w_out_odd"], io, x, li, 2, prm["g_post_mix"])
        act, upv, upg = _ffn_up_dec(x, mod_s, li, prm["g_pre_ffn"], prm["w_up"], prm["conv_w"], prm["conv_b"], conv2d)
        up = jnp.concatenate([upv, upg], axis=-1)
        new["conv"].append(jnp.stack([past["conv"][li][:, CONV_W - 2], up], axis=1))
        x = _outproj(tr, [act], prm["w_down"], li, x, li, 5, prm["g_post_ffn"])
    st = lambda t: jnp.stack(t, axis=0)
    return x.reshape(nb, 1, D_MODEL), tuple(st(new[k]) for k in ("ret", "s5r", "s5i", "kd", "vd", "ks", "vs", "conv"))


def _prepare_params(g_pre_mix, g_post_mix, g_pre_ffn, g_post_ffn, w_in_even, w_out_even, ret_gn_g, s5_lam_re, s5_lam_im,
                    s5_log_step, s5_b_re, s5_b_im, s5_c_re, s5_c_im, s5_d, s5_w_glu, w_in_odd, w_out_odd, diff_lam,
                    diff_subln_g, rel_bias, w_up, conv_w, conv_b, w_down):
    g3 = lambda g: g.reshape(g.shape[0], 1, g.shape[1])
    prm = dict(
        g_pre_mix=g3(g_pre_mix), g_post_mix=g3(g_post_mix), g_pre_ffn=g3(g_pre_ffn), g_post_ffn=g3(g_post_ffn),
        w_in_even=w_in_even, w_out_even=w_out_even, w_in_odd=w_in_odd, w_out_odd=w_out_odd, w_up=w_up, w_down=w_down,
        w_in_even_b=w_in_even.astype(bf16), w_out_even_b=w_out_even.astype(bf16), w_in_odd_b=w_in_odd.astype(bf16),
        w_out_odd_b=w_out_odd.astype(bf16), w_up_b=w_up.astype(bf16), w_down_b=w_down.astype(bf16),
        ret_gn_g=ret_gn_g, diff_lam=diff_lam, diff_subln_g=diff_subln_g, rel_bias=rel_bias,
        conv_w=conv_w, conv_b=g3(conv_b),
    )
    s5 = []
    for ie in range(s5_lam_re.shape[0]):
        a_re, a_im, bw, cw = _s5_params(s5_lam_re[ie], s5_lam_im[ie], s5_log_step[ie], s5_b_re[ie], s5_b_im[ie],
                                        s5_c_re[ie], s5_c_im[ie])
        s5.append(dict(a_re=a_re, a_im=a_im, bw=bw, cw=cw,
                       d=s5_d[ie].reshape(1, S5_WIDTH), glu=s5_w_glu[ie], glu_b=s5_w_glu[ie].astype(bf16)))
    prm["s5"] = s5
    return prm


def kernel(x_prompt, x_sample, c_prompt, c_sample, state_ret, state_s5_re, state_s5_im, cache_diff_k, cache_diff_v, cache_sb_k, cache_sb_v, state_conv, page_table, w_ada, b_ada, g_pre_mix, g_post_mix, g_pre_ffn, g_post_ffn, w_in_even, w_out_even, ret_gn_g, s5_lam_re, s5_lam_im, s5_log_step, s5_b_re, s5_b_im, s5_c_re, s5_c_im, s5_d, s5_w_glu, w_in_odd, w_out_odd, diff_lam, diff_subln_g, rel_bias, w_up, conv_w, conv_b, w_down):
    nb = x_sample.shape[0]
    bsz = x_prompt.shape[0]
    prm = _prepare_params(g_pre_mix, g_post_mix, g_pre_ffn, g_post_ffn, w_in_even, w_out_even, ret_gn_g, s5_lam_re,
                          s5_lam_im, s5_log_step, s5_b_re, s5_b_im, s5_c_re, s5_c_im, s5_d, s5_w_glu, w_in_odd, w_out_odd,
                          diff_lam, diff_subln_g, rel_bias, w_up, conv_w, conv_b, w_down)
    mod = _ada(jnp.concatenate([c_sample, c_prompt], axis=0), w_ada, b_ada)
    mod_s = mod[:, :nb]
    mod_p = mod[:, nb:].reshape(DEPTH, bsz, 1, 6 * D_MODEL)
    past = dict(ret=state_ret, s5_re=state_s5_re, s5_im=state_s5_im, diff_k=cache_diff_k, diff_v=cache_diff_v,
                sb_k=cache_sb_k, sb_v=cache_sb_v, conv=state_conv, page_table=page_table)
    y_prompt, sp = _prompt_trunk(x_prompt, mod_p, prm)
    y_sample, ss = _sample_trunk(x_sample, mod_s, prm, past)
    ret_p, s5r_p, s5i_p, kd_p, vd_p, ks_p, vs_p, conv_p = sp
    ret_s, s5r_s, s5i_s, kd_s, vd_s, ks_s, vs_s, conv_s = ss
    return (y_prompt, y_sample, ret_p, ret_s, s5r_p, s5r_s, s5i_p, s5i_s, kd_p, kd_s, vd_p, vd_s,
            ks_p, ks_s, vs_p, vs_s, conv_p, conv_s)
```

```python
import functools
import math

import jax
import jax.numpy as jnp
from jax import lax
from jax.experimental import pallas as pl
from jax.experimental.pallas import tpu as pltpu

f32 = jnp.float32
bf16 = jnp.bfloat16

D_MODEL = 1024
DEPTH = 4
PAST_LEN = 8192
PAGE_SIZE = 128
H_RET = 4
DK_RET = 128
DV_RET = 128
RET_CHUNK = 128
ROPE_BASE = 10000.0
S5_WIDTH = D_MODEL // 2
S5_GROUP = 16
S5_GROUPS = S5_WIDTH // S5_GROUP
S5_STATE = 64
S5_N = S5_GROUPS * S5_STATE
H_DIFF = 4
DH_DIFF = 64
H_SB = 8
DH_SB = 64
N_BUCKETS = 32
MAX_DISTANCE = 128
D_FF = 2816
CONV_W = 3
EPS = 1e-6
MIX_W = 512

LANES = 128
SUBLANES = 8
VMEM_LIMIT = 56 * 1024 * 1024
NEG_BIG = -1e30


def _cparams(n_axes):
    return pltpu.CompilerParams(dimension_semantics=("arbitrary",) * n_axes, vmem_limit_bytes=VMEM_LIMIT)


def _split(a):
    hi = a.astype(bf16)
    lo = (a - hi.astype(f32)).astype(bf16)
    return hi, lo


def _lhs(a, passes):
    if passes == 1:
        return (a.astype(bf16),)
    return _split(a.astype(f32))


def _dot(lhs, b, passes):
    if passes == 1:
        return jnp.dot(lhs[0], b.astype(bf16), preferred_element_type=f32)
    b_hi, b_lo = _split(b.astype(f32))
    return jnp.dot(lhs[0], b_hi, preferred_element_type=f32) + (
        jnp.dot(lhs[0], b_lo, preferred_element_type=f32) + jnp.dot(lhs[1], b_hi, preferred_element_type=f32)
    )


def _dot_nt(a, b):
    return lax.dot_general(a, b, (((1,), (1,)), ((), ())), preferred_element_type=f32)


def _norm_mod(x, g, sc, sh):
    ms = jnp.mean(x * x, axis=-1, keepdims=True)
    return (x * lax.rsqrt(ms + EPS) * g) * (1.0 + sc) + sh


def _gelu(x):
    return 0.5 * x * (1.0 + jnp.tanh(math.sqrt(2.0 / math.pi) * (x + 0.044715 * (x * x * x))))


def _sigmoid(x):
    return 1.0 / (1.0 + jnp.exp(-x))


def _silu(x):
    return x * _sigmoid(x)


def _ada_kernel(c_ref, w_ref, b_ref, o_ref):
    s = _silu(c_ref[...])
    o_ref[...] = _dot(_lhs(s, 3), w_ref[...], 3) + b_ref[...]


def _ada(c_all, w_ada, b_ada):
    n = c_all.shape[0]
    tn = 1024
    return pl.pallas_call(
        _ada_kernel,
        out_shape=jax.ShapeDtypeStruct((DEPTH, n, 6 * D_MODEL), f32),
        grid=(DEPTH, 6 * D_MODEL // tn),
        in_specs=[
            pl.BlockSpec((n, D_MODEL), lambda l, j: (0, 0)),
            pl.BlockSpec((None, D_MODEL, tn), lambda l, j: (l, 0, j)),
            pl.BlockSpec((None, 1, tn), lambda l, j: (l, 0, j)),
        ],
        out_specs=pl.BlockSpec((None, n, tn), lambda l, j: (l, 0, j)),
        compiler_params=_cparams(2),
        name="ada_mod",
    )(c_all, w_ada, b_ada.reshape(DEPTH, 1, 6 * D_MODEL))


class _Trunk:
    def __init__(self, mod, seq, tm, passes):
        self.mod = mod
        self.seq = seq
        self.tm = tm
        self.passes = passes
        self.per_row = seq == 1

    def mod_spec(self, li, k):
        if self.per_row:
            return pl.BlockSpec((None, self.tm, D_MODEL), lambda i: (li, i, k))
        tiles_per_seq = self.seq // self.tm
        return pl.BlockSpec((None, None, 1, D_MODEL), lambda i: (li, i // tiles_per_seq, 0, k))

    def row_spec(self, width):
        return pl.BlockSpec((self.tm, width), lambda i: (i, 0))


def _const_spec(shape, idx):
    return pl.BlockSpec(shape, lambda *_: idx)


def _rope(a, cos2, sin2):
    return a * cos2 + pltpu.roll(a, DK_RET // 2, 1) * sin2


def _proj_even_kernel(x_ref, sc_ref, sh_ref, g_ref, w_ref, cos_ref, sin_ref, q_ref, k_ref, v_ref, gg_ref, u_ref, *, passes):
    h = _lhs(_norm_mod(x_ref[...], g_ref[...], sc_ref[...], sh_ref[...]), passes)
    cos2 = cos_ref[...]
    sin2 = sin_ref[...]
    outs = (q_ref, k_ref, v_ref, gg_ref, u_ref)
    for n, ref in enumerate(outs):
        acc = _dot(h, w_ref[:, n * MIX_W:(n + 1) * MIX_W], passes)
        if n < 2:
            scale = 1.0 if n == 0 else DK_RET ** -0.5
            for hh in range(H_RET):
                sl = slice(hh * DK_RET, (hh + 1) * DK_RET)
                ref[:, sl] = (_rope(acc[:, sl], cos2, sin2) * scale).astype(ref.dtype)
        else:
            ref[...] = acc.astype(ref.dtype)


def _proj_even(tr, x, li, g_pre, w_in, cos2, sin2, out_dtypes):
    t = x.shape[0]
    tm = tr.tm
    n_in = w_in.shape[-1]
    if tr.per_row:
        cs_spec = pl.BlockSpec((1, DK_RET), lambda i: (0, 0))
    else:
        tiles = tr.seq // tm
        cs_spec = pl.BlockSpec((tm, DK_RET), lambda i: (i % tiles, 0))
    out_shapes = [jax.ShapeDtypeStruct((t, MIX_W), dt) for dt in out_dtypes]
    out_specs = [tr.row_spec(MIX_W) for _ in out_dtypes]
    if not tr.per_row:
        out_shapes[4] = jax.ShapeDtypeStruct((tr.seq, (t // tr.seq) * MIX_W), out_dtypes[4])
        out_specs[4] = pl.BlockSpec((tm, MIX_W), lambda i: (i % tiles, i // tiles))
    return pl.pallas_call(
        functools.partial(_proj_even_kernel, passes=tr.passes),
        out_shape=out_shapes,
        grid=(t // tm,),
        in_specs=[
            tr.row_spec(D_MODEL),
            tr.mod_spec(li, 1),
            tr.mod_spec(li, 0),
            _const_spec((None, 1, D_MODEL), (li, 0, 0)),
            _const_spec((None, D_MODEL, n_in), (li // 2, 0, 0)),
            cs_spec,
            cs_spec,
        ],
        out_specs=out_specs,
        compiler_params=_cparams(1),
        name="proj_even",
    )(x, tr.mod, tr.mod, g_pre, w_in, cos2, sin2)


def _proj_odd_kernel(x_ref, sc_ref, sh_ref, g_ref, w_ref, *out_refs, passes, with_copies):
    h = _lhs(_norm_mod(x_ref[...], g_ref[...], sc_ref[...], sh_ref[...]), passes)
    for n in range(6):
        acc = _dot(h, w_ref[:, n * MIX_W:(n + 1) * MIX_W], passes)
        if n in (0, 3):
            acc = acc * (DH_DIFF ** -0.5 if n == 0 else DH_SB ** -0.5)
        if with_copies and n in (1, 2):
            tm = acc.shape[0]
            hd = 2 * DH_DIFF
            for hh in range(H_DIFF):
                out_refs[n][pl.ds(hh, tm, stride=H_DIFF), :] = acc[:, hh * hd:(hh + 1) * hd]
        else:
            out_refs[n][...] = acc.astype(out_refs[n].dtype)
        if with_copies and n not in (0, 3):
            out_refs[6 + (n if n < 3 else n - 1) - 1][...] = acc.astype(bf16)


def _proj_odd(tr, x, li, g_pre, w_in, with_copies):
    t = x.shape[0]
    q_dt = bf16 if with_copies else f32
    dts = [q_dt, f32, f32, q_dt, f32, f32] + ([bf16] * 4 if with_copies else [])
    shapes = [(t, MIX_W)] * len(dts)
    specs = [tr.row_spec(MIX_W) for _ in dts]
    if with_copies:
        for n in (1, 2):
            shapes[n] = (t * H_DIFF, 2 * DH_DIFF)
            specs[n] = pl.BlockSpec((tr.tm * H_DIFF, 2 * DH_DIFF), lambda i: (i, 0))
    return pl.pallas_call(
        functools.partial(_proj_odd_kernel, passes=tr.passes, with_copies=with_copies),
        out_shape=[jax.ShapeDtypeStruct(s, dt) for s, dt in zip(shapes, dts)],
        grid=(t // tr.tm,),
        in_specs=[
            tr.row_spec(D_MODEL),
            tr.mod_spec(li, 1),
            tr.mod_spec(li, 0),
            _const_spec((None, 1, D_MODEL), (li, 0, 0)),
            _const_spec((None, D_MODEL, 6 * MIX_W), (li // 2, 0, 0)),
        ],
        out_specs=specs,
        compiler_params=_cparams(1),
        name="proj_odd",
    )(x, tr.mod, tr.mod, g_pre, w_in)


def _outproj_kernel(*refs, n_in, passes):
    a_refs = refs[:n_in]
    w_refs = refs[n_in:2 * n_in]
    x_ref, gate_ref, gp_ref, o_ref = refs[2 * n_in:]
    acc = _dot(_lhs(a_refs[0][...], passes), w_refs[0][...], passes)
    for a_ref, w_ref in zip(a_refs[1:], w_refs[1:]):
        acc = acc + _dot(_lhs(a_ref[...], passes), w_ref[...], passes)
    ms = jnp.mean(acc * acc, axis=-1, keepdims=True)
    y = acc * lax.rsqrt(ms + EPS) * gp_ref[...]
    o_ref[...] = x_ref[...] + gate_ref[...] * y


def _outproj(tr, acts, w, w_idx, x, li, gate_k, g_post):
    t = x.shape[0]
    widths = [a.shape[1] for a in acts]
    assert all(wd == widths[0] for wd in widths)
    w_specs = [pl.BlockSpec((None, widths[0], D_MODEL), lambda i, n=n: (w_idx, n, 0)) for n in range(len(acts))]
    return pl.pallas_call(
        functools.partial(_outproj_kernel, n_in=len(acts), passes=tr.passes),
        out_shape=jax.ShapeDtypeStruct((t, D_MODEL), f32),
        grid=(t // tr.tm,),
        in_specs=[tr.row_spec(wd) for wd in widths] + w_specs + [
            tr.row_spec(D_MODEL),
            tr.mod_spec(li, gate_k),
            _const_spec((None, 1, D_MODEL), (li, 0, 0)),
        ],
        out_specs=tr.row_spec(D_MODEL),
        compiler_params=_cparams(1),
        name="outproj",
    )(*acts, *([w] * len(acts)), x, tr.mod, g_post)


def _retention_kernel(q_ref, k_ref, v_ref, g_ref, intra_ref, qd_ref, kd_ref, gn_ref, o_ref, s_out_ref, s_scr, *, chunk_decay):
    c = pl.program_id(1)

    @pl.when(c == 0)
    def _():
        s_scr[...] = jnp.zeros_like(s_scr)

    heads = range(H_RET)
    sls = [slice(hh * DK_RET, (hh + 1) * DK_RET) for hh in heads]
    s_old = [s_scr[hh] for hh in heads]
    scores = [(_dot_nt(q_ref[:, sls[hh]], k_ref[:, sls[hh]]) * intra_ref[hh]).astype(bf16) for hh in heads]
    cross = [jnp.dot(q_ref[:, sls[hh]], s_old[hh].astype(bf16), preferred_element_type=f32) * qd_ref[hh] for hh in heads]
    kts = [(k_ref[:, sls[hh]].astype(f32) * kd_ref[hh]).T.astype(bf16) for hh in heads]
    outs = [jnp.dot(scores[hh], v_ref[:, sls[hh]], preferred_element_type=f32) + cross[hh] for hh in heads]
    for hh in heads:
        s_scr[hh] = s_old[hh] * chunk_decay[hh] + jnp.dot(kts[hh], v_ref[:, sls[hh]], preferred_element_type=f32)
    for hh in heads:
        sl = sls[hh]
        o = outs[hh]
        mu = jnp.mean(o, axis=-1, keepdims=True)
        d = o - mu
        var = jnp.mean(d * d, axis=-1, keepdims=True)
        on = d * lax.rsqrt(var + EPS) * gn_ref[hh]
        o_ref[:, sl] = (on * _silu(g_ref[:, sl])).astype(o_ref.dtype)

    @pl.when(c == pl.num_programs(1) - 1)
    def _():
        s_out_ref[...] = s_scr[...]


def _retention_tables(c):
    log_g = jnp.log1p(-jnp.exp2(-5.0 - jnp.arange(H_RET, dtype=f32)))
    idx = jnp.arange(c, dtype=f32)
    diff = idx[:, None] - idx[None, :]
    intra = jnp.where(diff >= 0, jnp.exp(jnp.maximum(diff, 0.0)[None] * log_g[:, None, None]), 0.0)
    q_decay = jnp.exp((idx + 1.0)[None, :] * log_g[:, None])[..., None]
    k_decay = jnp.exp((c - 1.0 - idx)[None, :] * log_g[:, None])[..., None]
    qd = jnp.broadcast_to(q_decay, (H_RET, c, DV_RET))
    kd = jnp.broadcast_to(k_decay, (H_RET, c, DK_RET))
    return intra, qd, kd


def _chunk_decay(c):
    return tuple(float(math.exp(c * math.log1p(-(2.0 ** (-5.0 - h))))) for h in range(H_RET))


def _retention(q, k, v, g, gn_g, bsz, seq):
    c = RET_CHUNK
    nc = seq // c
    intra, qd, kd = _retention_tables(c)
    row = pl.BlockSpec((c, MIX_W), lambda b, j: (b * nc + j, 0))
    tab = _const_spec((H_RET, c, c), (0, 0, 0))
    return pl.pallas_call(
        functools.partial(_retention_kernel, chunk_decay=_chunk_decay(c)),
        out_shape=[
            jax.ShapeDtypeStruct((bsz * seq, MIX_W), bf16),
            jax.ShapeDtypeStruct((bsz, H_RET, DK_RET, DV_RET), f32),
        ],
        grid=(bsz, nc),
        in_specs=[row, row, row, row, tab, tab, tab, _const_spec((H_RET, 1, DV_RET), (0, 0, 0))],
        out_specs=[row, pl.BlockSpec((None, H_RET, DK_RET, DV_RET), lambda b, j: (b, 0, 0, 0))],
        scratch_shapes=[pltpu.VMEM((H_RET, DK_RET, DV_RET), f32)],
        compiler_params=_cparams(2),
        name="retention",
    )(q, k, v, g, intra, qd, kd, gn_g.reshape(H_RET, 1, DV_RET))


U_GROUPS = S5_WIDTH // LANES
ST_PER_UG = S5_N // U_GROUPS


def _s5_params(lam_re, lam_im, log_step, b_re, b_im, c_re, c_im):
    dt = jnp.exp(log_step)[:, None]
    mag = jnp.exp(lam_re * dt)
    a_re, a_im = mag * jnp.cos(lam_im * dt), mag * jnp.sin(lam_im * dt)
    den = lam_re * lam_re + lam_im * lam_im
    coef_re = ((a_re - 1.0) * lam_re + a_im * lam_im) / den
    coef_im = (a_im * lam_re - (a_re - 1.0) * lam_im) / den
    bbar_re = coef_re[..., None] * b_re - coef_im[..., None] * b_im
    bbar_im = coef_re[..., None] * b_im + coef_im[..., None] * b_re
    gpb = LANES // S5_GROUP
    eye = jnp.eye(gpb, dtype=f32)

    def b_layout(bb):
        bb = bb.reshape(U_GROUPS, gpb, S5_STATE, S5_GROUP).transpose(0, 1, 3, 2)
        return jnp.einsum("jgip,gh->jgihp", bb, eye).reshape(U_GROUPS, LANES, ST_PER_UG)

    def c_layout(cc):
        cc = cc.reshape(U_GROUPS, gpb, S5_GROUP, S5_STATE)
        return jnp.einsum("jgip,gh->jgphi", cc, eye).reshape(U_GROUPS, ST_PER_UG, LANES)

    bw = jnp.concatenate([b_layout(bbar_re), b_layout(bbar_im)], axis=-1)
    cw = jnp.concatenate([c_layout(c_re), -c_layout(c_im)], axis=1)
    return a_re.reshape(1, S5_N), a_im.reshape(1, S5_N), bw, cw


S5_TS = 128
S5_PITCH = S5_TS + SUBLANES
N_SLAB = S5_N // LANES
SLABS_PER_UG = ST_PER_UG // LANES


def _s5_kernel(u_ref, bw_ref, a_ref, cw_ref, d_ref, glu_ref, y_ref, xr_out, xi_out, xr_scr, xi_scr, car_scr, *, bsz):
    j = pl.program_id(0)
    ts = S5_TS

    @pl.when(j == 0)
    def _():
        car_scr[...] = jnp.zeros_like(car_scr)

    for b in range(bsz):
        rows = slice(b * S5_PITCH, b * S5_PITCH + ts)
        ub = u_ref[:, b * S5_WIDTH:(b + 1) * S5_WIDTH].astype(bf16)
        for g in range(U_GROUPS):
            bu = jnp.dot(ub[:, g * LANES:(g + 1) * LANES], bw_ref[g], preferred_element_type=f32)
            for s in range(SLABS_PER_UG):
                xr_scr[g * SLABS_PER_UG + s, rows, :] = bu[:, s * LANES:(s + 1) * LANES]
                xi_scr[g * SLABS_PER_UG + s, rows, :] = bu[:, ST_PER_UG + s * LANES:ST_PER_UG + (s + 1) * LANES]

    def step(t, carry):
        new = []
        for s in range(N_SLAB):
            xr, xi = carry[2 * s], carry[2 * s + 1]
            ar = a_ref[0, s]
            ai = a_ref[1, s]
            at_t = pl.ds(t, bsz, stride=S5_PITCH)
            nr = (ar * xr - ai * xi) + xr_scr[s, at_t, :]
            ni = (ar * xi + ai * xr) + xi_scr[s, at_t, :]
            xr_scr[s, at_t, :] = nr
            xi_scr[s, at_t, :] = ni
            new += [nr, ni]
        return tuple(new)

    last = lax.fori_loop(0, ts, step, tuple(car_scr[k] for k in range(2 * N_SLAB)))
    for k in range(2 * N_SLAB):
        car_scr[k] = last[k]

    pre = []
    for b in range(bsz):
        rows = slice(b * S5_PITCH, b * S5_PITCH + ts)
        ys = []
        for g in range(U_GROUPS):
            slabs = range(g * SLABS_PER_UG, (g + 1) * SLABS_PER_UG)
            xr_g = jnp.concatenate([xr_scr[s, rows, :] for s in slabs], axis=1).astype(bf16)
            xi_g = jnp.concatenate([xi_scr[s, rows, :] for s in slabs], axis=1).astype(bf16)
            ys.append(jnp.dot(xr_g, cw_ref[g, :ST_PER_UG, :], preferred_element_type=f32)
                      + jnp.dot(xi_g, cw_ref[g, ST_PER_UG:, :], preferred_element_type=f32))
        pre.append(jnp.concatenate(ys, axis=1))
    acts = [_gelu(pre[b] + d_ref[...] * u_ref[:, b * S5_WIDTH:(b + 1) * S5_WIDTH]) for b in range(bsz)]
    gates = [jnp.dot(acts[b].astype(bf16), glu_ref[...], preferred_element_type=f32) for b in range(bsz)]
    for b in range(bsz):
        y_ref[b] = (acts[b] * _sigmoid(gates[b])).astype(y_ref.dtype)

    @pl.when(j == pl.num_programs(0) - 1)
    def _():
        for s in range(N_SLAB):
            xr_out[:, s * LANES:(s + 1) * LANES] = last[2 * s]
            xi_out[:, s * LANES:(s + 1) * LANES] = last[2 * s + 1]


def _s5(u_tb, bw, a_re, a_im, cw, d_skip, w_glu, bsz, seq):
    ts = S5_TS
    a_tab = jnp.stack([a_re, a_im]).reshape(2, N_SLAB, 1, LANES)
    a_tab = jnp.broadcast_to(a_tab, (2, N_SLAB, bsz, LANES))
    st = _const_spec((bsz, S5_N), (0, 0))
    y, xr, xi = pl.pallas_call(
        functools.partial(_s5_kernel, bsz=bsz),
        out_shape=[
            jax.ShapeDtypeStruct((bsz, seq, S5_WIDTH), bf16),
            jax.ShapeDtypeStruct((bsz, S5_N), f32),
            jax.ShapeDtypeStruct((bsz, S5_N), f32),
        ],
        grid=(seq // ts,),
        in_specs=[
            pl.BlockSpec((ts, bsz * S5_WIDTH), lambda j: (j, 0)),
            _const_spec((U_GROUPS, LANES, 2 * ST_PER_UG), (0, 0, 0)),
            _const_spec((2, N_SLAB, bsz, LANES), (0, 0, 0, 0)),
            _const_spec((U_GROUPS, 2 * ST_PER_UG, LANES), (0, 0, 0)),
            _const_spec((1, S5_WIDTH), (0, 0)),
            _const_spec((S5_WIDTH, S5_WIDTH), (0, 0)),
        ],
        out_specs=[pl.BlockSpec((bsz, ts, S5_WIDTH), lambda j: (0, j, 0)), st, st],
        scratch_shapes=[
            pltpu.VMEM((N_SLAB, bsz * S5_PITCH, LANES), f32),
            pltpu.VMEM((N_SLAB, bsz * S5_PITCH, LANES), f32),
            pltpu.VMEM((2 * N_SLAB, bsz, LANES), f32),
        ],
        compiler_params=_cparams(1),
        name="s5",
    )(u_tb, bw, a_tab, cw, d_skip, w_glu)
    return y.reshape(bsz * seq, S5_WIDTH), xr.reshape(bsz, S5_GROUPS, S5_STATE), xi.reshape(bsz, S5_GROUPS, S5_STATE)


FF_CHUNK = 256
FFN_HALO = SUBLANES


def _ffn_up_kernel(x_ref, xh_ref, sc_ref, sh_ref, g_ref, w_ref, cw_ref, cb_ref, act_ref, cs_ref, ext_scr, *, tiles_per_seq):
    i = pl.program_id(0)
    tm = x_ref.shape[0]
    g = g_ref[...]
    sc = sc_ref[...]
    sh = sh_ref[...]
    h = _norm_mod(x_ref[...], g, sc, sh).astype(bf16)
    hh = _norm_mod(xh_ref[...], g, sc, sh).astype(bf16)
    keep = (i % tiles_per_seq != 0).astype(f32)
    for n in range(D_FF // FF_CHUNK):
        mixed = []
        for half in range(2):
            cs = slice(half * D_FF + n * FF_CHUNK, half * D_FF + (n + 1) * FF_CHUNK)
            w = w_ref[:, cs]
            up = jnp.dot(h, w, preferred_element_type=f32)
            halo = jnp.dot(hh, w, preferred_element_type=f32) * keep
            ext_scr[half, 0:FFN_HALO, :] = halo
            ext_scr[half, FFN_HALO:, :] = up
            r1 = ext_scr[half, pl.ds(FFN_HALO - 1, tm), :]
            r2 = ext_scr[half, pl.ds(FFN_HALO - 2, tm), :]
            mixed.append(cw_ref[0:1, cs] * r2 + cw_ref[1:2, cs] * r1 + cw_ref[2:3, cs] * up + cb_ref[:, cs])
            cs_ref[:, cs] = up[tm - (CONV_W - 1):, :]
        act_ref[:, n * FF_CHUNK:(n + 1) * FF_CHUNK] = (mixed[0] * _gelu(mixed[1])).astype(act_ref.dtype)


def _ffn_up(tr, x, li, g_pre, w_up, conv_w, conv_b, bsz):
    t = x.shape[0]
    tm = tr.tm
    tiles = tr.seq // tm
    halo_blocks = tm // FFN_HALO
    return pl.pallas_call(
        functools.partial(_ffn_up_kernel, tiles_per_seq=tiles),
        out_shape=[
            jax.ShapeDtypeStruct((t, D_FF), bf16),
            jax.ShapeDtypeStruct((bsz, CONV_W - 1, 2 * D_FF), f32),
        ],
        grid=(t // tm,),
        in_specs=[
            tr.row_spec(D_MODEL),
            pl.BlockSpec((FFN_HALO, D_MODEL), lambda i: (jnp.maximum(i * halo_blocks - 1, 0), 0)),
            tr.mod_spec(li, 4),
            tr.mod_spec(li, 3),
            _const_spec((None, 1, D_MODEL), (li, 0, 0)),
            _const_spec((None, D_MODEL, 2 * D_FF), (li, 0, 0)),
            _const_spec((None, CONV_W, 2 * D_FF), (li, 0, 0)),
            _const_spec((None, 1, 2 * D_FF), (li, 0, 0)),
        ],
        out_specs=[
            tr.row_spec(D_FF),
            pl.BlockSpec((None, CONV_W - 1, 2 * D_FF), lambda i: (i // tiles, 0, 0)),
        ],
        scratch_shapes=[pltpu.VMEM((2, tm + FFN_HALO, FF_CHUNK), f32)],
        compiler_params=_cparams(1),
        name="ffn_up",
    )(x, x, tr.mod, tr.mod, g_pre, w_up, conv_w, conv_b)


def _t5_bucket(dist):
    n = jnp.maximum(dist, 0)
    max_exact = N_BUCKETS // 2
    large = max_exact + (jnp.log(jnp.maximum(n, 1).astype(f32) / max_exact)
                         / math.log(MAX_DISTANCE / max_exact) * (N_BUCKETS - max_exact)).astype(jnp.int32)
    large = jnp.minimum(large, N_BUCKETS - 1)
    return jnp.where(n < max_exact, n, large)


def _bias_by_distance(rel_bias, n):
    return rel_bias[_t5_bucket(jnp.arange(n))]


ATT_T = 512
SUFFIX_W = 256
SB_HEAD_GROUP = 4


def _diff_tile(q_ref, k_ref, v_ref, bias_ref, m_scr, l_scr, acc_scr, diagonal):
    t = q_ref.shape[0]
    if diagonal:
        causal = lax.broadcasted_iota(jnp.int32, (t, t), 0) >= lax.broadcasted_iota(jnp.int32, (t, t), 1)
    streams = range(2 * H_DIFF)
    scores = []
    for n in streams:
        sl = slice(n * DH_DIFF, (n + 1) * DH_DIFF)
        s = _dot_nt(q_ref[:, sl], k_ref[:, sl]) + bias_ref[n // 2]
        scores.append(jnp.where(causal, s, NEG_BIG) if diagonal else s)
    probs, alphas = [], []
    for n in streams:
        m_old = m_scr[n]
        m_new = jnp.maximum(m_old, jnp.max(scores[n], axis=-1, keepdims=True))
        p = jnp.exp(scores[n] - m_new)
        alpha = jnp.exp(m_old - m_new)
        l_scr[n] = alpha * l_scr[n] + jnp.sum(p, axis=-1, keepdims=True)
        m_scr[n] = m_new
        probs.append(p.astype(bf16))
        alphas.append(alpha)
    for n in streams:
        vh = v_ref[:, (n // 2) * 2 * DH_DIFF:(n // 2 + 1) * 2 * DH_DIFF]
        acc_scr[n] = alphas[n] * acc_scr[n] + jnp.dot(probs[n], vh, preferred_element_type=f32)


def _diff_attn_kernel(lam_ref, q_ref, k_ref, v_ref, bias_ref, sg_ref, o_ref, m_scr, l_scr, acc_scr, *, out_scale):
    qi = pl.program_id(1)
    kj = pl.program_id(2)

    @pl.when(kj == 0)
    def _():
        m_scr[...] = jnp.full_like(m_scr, NEG_BIG)
        l_scr[...] = jnp.zeros_like(l_scr)
        acc_scr[...] = jnp.zeros_like(acc_scr)

    @pl.when(kj < qi)
    def _():
        _diff_tile(q_ref, k_ref, v_ref, bias_ref, m_scr, l_scr, acc_scr, False)

    @pl.when(kj == qi)
    def _():
        _diff_tile(q_ref, k_ref, v_ref, bias_ref, m_scr, l_scr, acc_scr, True)
        lam = lam_ref[0]
        for hh in range(H_DIFF):
            o = acc_scr[2 * hh] / l_scr[2 * hh] - lam * (acc_scr[2 * hh + 1] / l_scr[2 * hh + 1])
            ms = jnp.mean(o * o, axis=-1, keepdims=True)
            o = o * lax.rsqrt(ms + EPS) * sg_ref[...] * out_scale
            o_ref[:, hh * 2 * DH_DIFF:(hh + 1) * 2 * DH_DIFF] = o.astype(o_ref.dtype)


def _diff_attn(q, k, v, bias_tiles, lam, subln_g, lam_init, bsz, seq):
    t = ATT_T
    nq = seq // t
    qspec = pl.BlockSpec((t, MIX_W), lambda b, i, j: (b * nq + i, 0))
    kspec = pl.BlockSpec((t, MIX_W), lambda b, i, j: (b * nq + jnp.minimum(j, i), 0))
    return pl.pallas_call(
        functools.partial(_diff_attn_kernel, out_scale=1.0 - lam_init),
        out_shape=jax.ShapeDtypeStruct((bsz * seq, MIX_W), bf16),
        grid=(bsz, nq, nq),
        in_specs=[
            pl.BlockSpec(memory_space=pltpu.SMEM),
            qspec,
            kspec,
            kspec,
            pl.BlockSpec((H_DIFF, None, t, t), lambda b, i, j: (0, jnp.maximum(i - j, 0), 0, 0)),
            _const_spec((1, 2 * DH_DIFF), (0, 0)),
        ],
        out_specs=qspec,
        scratch_shapes=[
            pltpu.VMEM((2 * H_DIFF, t, 1), f32),
            pltpu.VMEM((2 * H_DIFF, t, 1), f32),
            pltpu.VMEM((2 * H_DIFF, t, 2 * DH_DIFF), f32),
        ],
        compiler_params=_cparams(3),
        name="diff_attn",
    )(lam, q, k, v, bias_tiles, subln_g)


def _log_sigmoid_pair(z):
    ls = jnp.minimum(z, 0.0) - jnp.log(1.0 + jnp.exp(-jnp.abs(z)))
    return ls, ls - z


def _suffix_sum(lk, upper):
    hi, lo = _split(lk)
    return jnp.dot(hi, upper, preferred_element_type=f32) + jnp.dot(lo, upper, preferred_element_type=f32)


def _upper_ones(t):
    j = lax.broadcasted_iota(jnp.int32, (t, t), 0)
    k = lax.broadcasted_iota(jnp.int32, (t, t), 1)
    return jnp.where(j > k, 1.0, 0.0).astype(bf16)


def _sb_attn_kernel(q_ref, k_ref, v_ref, o_ref, r_scr, acc_scr):
    qi = pl.program_id(1)
    st = pl.program_id(2)
    t = q_ref.shape[0]

    @pl.when(st == 0)
    def _():
        r_scr[...] = jnp.zeros_like(r_scr)
        acc_scr[...] = jnp.zeros_like(acc_scr)

    def tile(diagonal):
        if diagonal:
            strict = lax.broadcasted_iota(jnp.int32, (t, t), 1) < lax.broadcasted_iota(jnp.int32, (t, t), 0)
        upper = _upper_ones(SUFFIX_W)
        nblk = t // SUFFIX_W
        for g0 in range(0, H_SB, SB_HEAD_GROUP):
            heads = range(g0, g0 + SB_HEAD_GROUP)
            pairs = {}
            for hh in heads:
                sl = slice(hh * DH_SB, (hh + 1) * DH_SB)
                ls, lk = _log_sigmoid_pair(_dot_nt(q_ref[:, sl], k_ref[:, sl]))
                pairs[hh] = (ls, jnp.where(strict, lk, 0.0) if diagonal else lk)
            sufs = {hh: [_suffix_sum(pairs[hh][1][:, c * SUFFIX_W:(c + 1) * SUFFIX_W], upper) for c in range(nblk)]
                    for hh in heads}
            ws = {}
            for hh in heads:
                ls, lk = pairs[hh]
                carry = r_scr[hh]
                afters = [None] * nblk
                for c in reversed(range(nblk)):
                    afters[c] = sufs[hh][c] + carry
                    carry = carry + jnp.sum(lk[:, c * SUFFIX_W:(c + 1) * SUFFIX_W], axis=-1, keepdims=True)
                r_scr[hh] = carry
                w = jnp.exp(ls + jnp.concatenate(afters, axis=1))
                ws[hh] = (jnp.where(strict, w, 0.0) if diagonal else w).astype(bf16)
            for hh in heads:
                sl = slice(hh * DH_SB, (hh + 1) * DH_SB)
                acc_scr[hh] = acc_scr[hh] + jnp.dot(ws[hh], v_ref[:, sl], preferred_element_type=f32)

    @pl.when(st == 0)
    def _():
        tile(True)

    @pl.when(jnp.logical_and(st > 0, st <= qi))
    def _():
        tile(False)

    @pl.when(st == qi)
    def _():
        for hh in range(H_SB):
            o_ref[:, hh * DH_SB:(hh + 1) * DH_SB] = acc_scr[hh].astype(o_ref.dtype)


def _sb_attn(q, k, v, bsz, seq):
    t = ATT_T
    nq = seq // t
    qspec = pl.BlockSpec((t, MIX_W), lambda b, i, s: (b * nq + i, 0))
    kspec = pl.BlockSpec((t, MIX_W), lambda b, i, s: (b * nq + jnp.maximum(i - s, 0), 0))
    return pl.pallas_call(
        _sb_attn_kernel,
        out_shape=jax.ShapeDtypeStruct((bsz * seq, MIX_W), bf16),
        grid=(bsz, nq, nq),
        in_specs=[qspec, kspec, kspec],
        out_specs=qspec,
        scratch_shapes=[pltpu.VMEM((H_SB, t, 1), f32), pltpu.VMEM((H_SB, t, DH_SB), f32)],
        compiler_params=_cparams(3),
        name="sb_attn",
    )(q, k, v)


DEC_BB = 8


def _even_dec_kernel(q_ref, k_ref, v_ref, g_ref, u_ref, s0_ref, x0r_ref, x0i_ref, gn_ref, bw_ref, are_ref, aim_ref,
                     cw_ref, d_ref, glu_ref, o_ref, y_ref, s_ref, xr_ref, xi_ref, *, decay):
    pad = jnp.zeros((DK_RET - DEC_BB, DK_RET), f32)
    for hh in range(H_RET):
        sl = slice(hh * DK_RET, (hh + 1) * DK_RET)
        qh = q_ref[:, sl]
        kh = k_ref[:, sl]
        vh = v_ref[:, sl]
        q_t = jnp.concatenate([qh, pad], axis=0).T
        k_t = jnp.concatenate([kh, pad], axis=0).T
        qk = jnp.sum(qh * kh, axis=-1, keepdims=True)
        rows = []
        for b in range(DEC_BB):
            s0 = s0_ref[b, hh]
            qs = jnp.sum(q_t[:, b:b + 1] * s0, axis=0, keepdims=True)
            rows.append(qk[b:b + 1] * vh[b:b + 1] + qs * decay[hh])
            s_ref[b, hh] = s0 * decay[hh] + k_t[:, b:b + 1] * vh[b:b + 1]
        o = jnp.concatenate(rows, axis=0)
        mu = jnp.mean(o, axis=-1, keepdims=True)
        dlt = o - mu
        var = jnp.mean(dlt * dlt, axis=-1, keepdims=True)
        on = dlt * lax.rsqrt(var + EPS) * gn_ref[hh]
        o_ref[:, sl] = on * _silu(g_ref[:, sl])

    u = u_ref[...]
    bus = [_dot(_lhs(u[:, j * LANES:(j + 1) * LANES], 3), bw_ref[j], 3) for j in range(U_GROUPS)]
    bu_re = jnp.concatenate([b[:, :ST_PER_UG] for b in bus], axis=1)
    bu_im = jnp.concatenate([b[:, ST_PER_UG:] for b in bus], axis=1)
    are = are_ref[...]
    aim = aim_ref[...]
    x0r = x0r_ref[...]
    x0i = x0i_ref[...]
    xr = (are * x0r - aim * x0i) + bu_re
    xi = (are * x0i + aim * x0r) + bu_im
    xr_ref[...] = xr
    xi_ref[...] = xi
    ys = []
    for j in range(U_GROUPS):
        ss = slice(j * ST_PER_UG, (j + 1) * ST_PER_UG)
        ys.append(_dot(_lhs(xr[:, ss], 3), cw_ref[j, :ST_PER_UG, :], 3) + _dot(_lhs(xi[:, ss], 3), cw_ref[j, ST_PER_UG:, :], 3))
    y = _gelu(jnp.concatenate(ys, axis=1) + d_ref[...] * u)
    z = _dot(_lhs(y, 3), glu_ref[...], 3)
    y_ref[...] = y * _sigmoid(z)


def _even_dec(q, k, v, g, u, state_ret, s5_re, s5_im, ie, gn_g, bw, a_re, a_im, cw, d_skip, w_glu):
    nb = q.shape[0]
    row = pl.BlockSpec((DEC_BB, MIX_W), lambda i: (i, 0))
    st_spec = pl.BlockSpec((None, DEC_BB, H_RET, DK_RET, DV_RET), lambda i: (ie, i, 0, 0, 0))
    x_spec = pl.BlockSpec((None, DEC_BB, S5_N), lambda i: (ie, i, 0))
    decay = tuple(float(math.exp(math.log1p(-(2.0 ** (-5.0 - h))))) for h in range(H_RET))
    return pl.pallas_call(
        functools.partial(_even_dec_kernel, decay=decay),
        out_shape=[
            jax.ShapeDtypeStruct((nb, MIX_W), f32),
            jax.ShapeDtypeStruct((nb, S5_WIDTH), f32),
            jax.ShapeDtypeStruct((nb, H_RET, DK_RET, DV_RET), f32),
            jax.ShapeDtypeStruct((nb, S5_N), f32),
            jax.ShapeDtypeStruct((nb, S5_N), f32),
        ],
        grid=(nb // DEC_BB,),
        in_specs=[
            row, row, row, row, row, st_spec, x_spec, x_spec,
            _const_spec((H_RET, 1, DV_RET), (0, 0, 0)),
            _const_spec((U_GROUPS, LANES, 2 * ST_PER_UG), (0, 0, 0)),
            _const_spec((1, S5_N), (0, 0)),
            _const_spec((1, S5_N), (0, 0)),
            _const_spec((U_GROUPS, 2 * ST_PER_UG, LANES), (0, 0, 0)),
            _const_spec((1, S5_WIDTH), (0, 0)),
            _const_spec((S5_WIDTH, S5_WIDTH), (0, 0)),
        ],
        out_specs=[
            row, row,
            pl.BlockSpec((DEC_BB, H_RET, DK_RET, DV_RET), lambda i: (i, 0, 0, 0)),
            pl.BlockSpec((DEC_BB, S5_N), lambda i: (i, 0)),
            pl.BlockSpec((DEC_BB, S5_N), lambda i: (i, 0)),
        ],
        compiler_params=_cparams(1),
        name="even_dec",
    )(q, k, v, g, u, state_ret, s5_re, s5_im, gn_g.reshape(H_RET, 1, DV_RET), bw, a_re, a_im, cw, d_skip, w_glu)


def _ffn_up_dec_kernel(x_ref, sc_ref, sh_ref, g_ref, wv_ref, wg_ref, b0v_ref, b0g_ref, b1v_ref, b1g_ref,
                       cwv_ref, cwg_ref, cbv_ref, cbg_ref, act_ref, upv_ref, upg_ref):
    h = _lhs(_norm_mod(x_ref[...], g_ref[...], sc_ref[...], sh_ref[...]), 3)
    upv = _dot(h, wv_ref[...], 3)
    upg = _dot(h, wg_ref[...], 3)
    val = cwv_ref[0:1, :] * b0v_ref[...] + cwv_ref[1:2, :] * b1v_ref[...] + cwv_ref[2:3, :] * upv + cbv_ref[...]
    gate = cwg_ref[0:1, :] * b0g_ref[...] + cwg_ref[1:2, :] * b1g_ref[...] + cwg_ref[2:3, :] * upg + cbg_ref[...]
    act_ref[...] = val * _gelu(gate)
    upv_ref[...] = upv
    upg_ref[...] = upg


def _ffn_up_dec(x, mod_s, li, g_pre, w_up, conv_w, conv_b, state_conv2d):
    nb = x.shape[0]
    nch = D_FF // FF_CHUNK
    full = _const_spec((nb, D_MODEL), (0, 0))

    def mod(k):
        return pl.BlockSpec((None, nb, D_MODEL), lambda n: (li, 0, k))

    def cols(rows, off):
        return pl.BlockSpec((None, rows, FF_CHUNK), lambda n: (li, 0, n + off))

    out = pl.BlockSpec((nb, FF_CHUNK), lambda n: (0, n))
    return pl.pallas_call(
        _ffn_up_dec_kernel,
        out_shape=[jax.ShapeDtypeStruct((nb, D_FF), f32)] * 3,
        grid=(nch,),
        in_specs=[
            full, mod(4), mod(3), _const_spec((None, 1, D_MODEL), (li, 0, 0)),
            cols(D_MODEL, 0), cols(D_MODEL, nch),
            cols(nb, 0), cols(nb, nch), cols(nb, 2 * nch), cols(nb, 3 * nch),
            cols(CONV_W, 0), cols(CONV_W, nch), cols(1, 0), cols(1, nch),
        ],
        out_specs=[out, out, out],
        compiler_params=_cparams(1),
        name="ffn_up_dec",
    )(x, mod_s, mod_s, g_pre, w_up, w_up, state_conv2d, state_conv2d, state_conv2d, state_conv2d,
      conv_w, conv_w, conv_b, conv_b)


DEC_PP = 8


def _attn_dec_kernel(pt_ref, lam_ref, qd_ref, qs_ref, kn_ref, vn_ref, bias_ref, bias0_ref, sg_ref, *rest, n_pages, out_scale):
    del pt_ref
    pages = rest[:4 * DEC_PP]
    o_ref, md, ld, accd, rs, accs = rest[4 * DEC_PP:]
    j = pl.program_id(1)
    rows = H_SB
    hd = 2 * DH_DIFF
    head_of_lane = lax.broadcasted_iota(jnp.int32, (rows, MIX_W), 1) // DH_SB
    q_sb = jnp.where(head_of_lane == lax.broadcasted_iota(jnp.int32, (rows, MIX_W), 0), qs_ref[...], 0.0)

    def per_row_head(ref):
        return jnp.concatenate([ref[:, (r // 2) * hd:(r // 2 + 1) * hd] for r in range(rows)], axis=0)

    half_of_lane = lax.broadcasted_iota(jnp.int32, (rows, hd), 1) // DH_DIFF
    row_id = lax.broadcasted_iota(jnp.int32, (rows, hd), 0)
    q_diff = jnp.where(half_of_lane == row_id % 2, per_row_head(qd_ref), 0.0)

    @pl.when(j == 0)
    def _():
        md[...] = jnp.sum(q_diff * per_row_head(kn_ref), axis=-1, keepdims=True) + bias0_ref[...]
        ld[...] = jnp.ones_like(ld)
        accd[...] = per_row_head(vn_ref)
        rs[...] = jnp.zeros_like(rs)
        accs[...] = jnp.zeros_like(accs)

    qd_b = q_diff.astype(bf16)
    qs_b = q_sb.astype(bf16)
    upper = _upper_ones(PAGE_SIZE)

    flat = PAGE_SIZE * H_DIFF
    own = lax.broadcasted_iota(jnp.int32, (rows, flat), 1) % H_DIFF == lax.broadcasted_iota(jnp.int32, (rows, flat), 0) // 2
    scores = []
    for i in range(DEC_PP):
        page = n_pages - 1 - (j * DEC_PP + i)
        s = _dot_nt(qd_b, pages[4 * i][...].astype(bf16)) + bias_ref[page]
        scores.append(jnp.where(own, s, NEG_BIG))
    m_old = md[...]
    m_new = m_old
    for s in scores:
        m_new = jnp.maximum(m_new, jnp.max(s, axis=-1, keepdims=True))
    alpha = jnp.exp(m_old - m_new)
    l_new = alpha * ld[...]
    acc_d = alpha * accd[...]
    for i, s in enumerate(scores):
        p = jnp.exp(s - m_new)
        l_new = l_new + jnp.sum(p, axis=-1, keepdims=True)
        acc_d = acc_d + jnp.dot(p.astype(bf16), pages[4 * i + 1][...].astype(bf16), preferred_element_type=f32)
    ld[...] = l_new
    md[...] = m_new
    accd[...] = acc_d

    zs = [jnp.dot(qs_b, pages[4 * i + 2][...].reshape(MIX_W, PAGE_SIZE).astype(bf16), preferred_element_type=f32)
          for i in range(DEC_PP)]
    pairs = [_log_sigmoid_pair(z) for z in zs]
    sufs = [_suffix_sum(lk, upper) for _, lk in pairs]
    carry = rs[...]
    ws = []
    for (ls, lk), suf in zip(pairs, sufs):
        ws.append(jnp.exp(ls + (suf + carry)).astype(bf16))
        carry = carry + jnp.sum(lk, axis=-1, keepdims=True)
    acc_s = accs[...]
    for i, w in enumerate(ws):
        acc_s = acc_s + _dot_nt(w, pages[4 * i + 3][...].reshape(MIX_W, PAGE_SIZE).astype(bf16))
    rs[...] = carry
    accs[...] = acc_s

    @pl.when(j == pl.num_programs(1) - 1)
    def _():
        lam = lam_ref[0]
        acc = accd[...] / ld[...]
        pieces = []
        for hh in range(H_DIFF):
            o = acc[2 * hh:2 * hh + 1] - lam * acc[2 * hh + 1:2 * hh + 2]
            ms = jnp.mean(o * o, axis=-1, keepdims=True)
            pieces.append(o * lax.rsqrt(ms + EPS) * sg_ref[...] * out_scale)
        a_sb = accs[...]
        for hh in range(H_SB):
            pieces.append(a_sb[hh:hh + 1, hh * DH_SB:(hh + 1) * DH_SB])
        o_ref[...] = jnp.concatenate(pieces, axis=1)


def _attn_dec(page_table, lam, qd, qs, k_new, v_new, bias_pages, bias0, subln_g, caches, io, lam_init):
    nb, n_pages = page_table.shape
    row3 = lambda a: a.reshape(nb, 1, a.shape[-1])
    rspec = pl.BlockSpec((None, 1, MIX_W), lambda b, j, pt: (b, 0, 0))

    def page_spec(i, cache):
        blk = (None, None) + cache.shape[2:]
        zeros = (0,) * (cache.ndim - 2)
        return pl.BlockSpec(blk, lambda b, j, pt: (io, pt[b, n_pages - 1 - (j * DEC_PP + i)]) + zeros)

    page_specs, page_args = [], []
    for i in range(DEC_PP):
        for cache in caches:
            page_specs.append(page_spec(i, cache))
            page_args.append(cache)
    grid_spec = pltpu.PrefetchScalarGridSpec(
        num_scalar_prefetch=1,
        grid=(nb, n_pages // DEC_PP),
        in_specs=[
            pl.BlockSpec(memory_space=pltpu.SMEM),
            rspec, rspec, rspec, rspec,
            pl.BlockSpec((n_pages, 1, PAGE_SIZE * H_DIFF), lambda b, j, pt: (0, 0, 0)),
            pl.BlockSpec((H_SB, 1), lambda b, j, pt: (0, 0)),
            pl.BlockSpec((1, 2 * DH_DIFF), lambda b, j, pt: (0, 0)),
        ] + page_specs,
        out_specs=pl.BlockSpec((None, 1, 2 * MIX_W), lambda b, j, pt: (b, 0, 0)),
        scratch_shapes=[
            pltpu.VMEM((H_SB, 1), f32), pltpu.VMEM((H_SB, 1), f32), pltpu.VMEM((H_SB, 2 * DH_DIFF), f32),
            pltpu.VMEM((H_SB, 1), f32), pltpu.VMEM((H_SB, MIX_W), f32),
        ],
    )
    out = pl.pallas_call(
        functools.partial(_attn_dec_kernel, n_pages=n_pages, out_scale=1.0 - lam_init),
        out_shape=jax.ShapeDtypeStruct((nb, 1, 2 * MIX_W), f32),
        grid_spec=grid_spec,
        compiler_params=_cparams(2),
        name="attn_dec",
    )(page_table, lam, row3(qd), row3(qs), row3(k_new), row3(v_new), bias_pages, bias0, subln_g, *page_args)
    return out.reshape(nb, 2 * MIX_W)


def _lam_init(li):
    return 0.8 - 0.6 * math.exp(-0.3 * li)


def _diff_lambda(lam_vecs, li):
    lv = lam_vecs.astype(f32)
    lam = jnp.exp(jnp.dot(lv[0], lv[1])) - jnp.exp(jnp.dot(lv[2], lv[3])) + _lam_init(li)
    return lam.reshape(1)


def _rope_tables(pos):
    half = DK_RET // 2
    inv = jnp.power(ROPE_BASE, -jnp.arange(half, dtype=f32) / half)
    ang = pos.astype(f32)[:, None] * inv[None, :]
    cos, sin = jnp.cos(ang), jnp.sin(ang)
    return jnp.concatenate([cos, cos], axis=-1), jnp.concatenate([-sin, sin], axis=-1)


def _prompt_bias_tiles(bias_d, seq):
    t = ATT_T
    nd = seq // t
    m = jnp.arange(2 * t)
    r_minus_c = jnp.where(m < t, -m, 2 * t - m)
    dist = jnp.arange(nd)[:, None] * t + r_minus_c[None, :]
    gen = jnp.transpose(bias_d[jnp.clip(dist, 0, seq - 1)], (2, 0, 1))
    rows = jnp.tile(gen, (1, 1, t))[:, :, :t * (2 * t - 1)].reshape(gen.shape[0], nd, t, 2 * t - 1)
    return rows[:, :, :, :t]


def _prompt_trunk(x_prompt, mod_p, prm):
    bsz, seq, _ = x_prompt.shape
    tr = _Trunk(mod_p, seq, min(1024, seq), 1)
    tr_ffn = _Trunk(mod_p, seq, min(512, seq), 1)
    x = x_prompt.reshape(bsz * seq, D_MODEL)
    cos2, sin2 = _rope_tables(jnp.arange(seq))
    bias_tiles = _prompt_bias_tiles(_bias_by_distance(prm["rel_bias"], seq), seq)
    new = {k: [] for k in ("ret", "s5r", "s5i", "kd", "vd", "ks", "vs", "conv")}
    for li in range(DEPTH):
        if li % 2 == 0:
            ie = li // 2
            s5p = prm["s5"][ie]
            q, k, v, g, u = _proj_even(tr, x, li, prm["g_pre_mix"], prm["w_in_even_b"], cos2, sin2, (bf16, bf16, bf16, f32, f32))
            o, ret_s = _retention(q, k, v, g, prm["ret_gn_g"][ie], bsz, seq)
            y, xr, xi = _s5(u, s5p["bw"].astype(bf16), s5p["a_re"], s5p["a_im"], s5p["cw"].astype(bf16), s5p["d"],
                            s5p["glu_b"], bsz, seq)
            new["ret"].append(ret_s)
            new["s5r"].append(xr)
            new["s5i"].append(xi)
            x = _outproj(tr, [o, y], prm["w_out_even_b"], ie, x, li, 2, prm["g_post_mix"])
        else:
            io = li // 2
            dq, dk, dv, sq, sk, sv, dk_b, dv_b, sk_b, sv_b = _proj_odd(tr, x, li, prm["g_pre_mix"], prm["w_in_odd_b"], True)
            lam = _diff_lambda(prm["diff_lam"][io], li)
            od = _diff_attn(dq, dk_b, dv_b, bias_tiles, lam, prm["diff_subln_g"][io].reshape(1, -1), _lam_init(li), bsz, seq)
            osb = _sb_attn(sq, sk_b, sv_b, bsz, seq)
            new["kd"].append(dk.reshape(bsz, seq, H_DIFF, 2 * DH_DIFF))
            new["vd"].append(dv.reshape(bsz, seq, H_DIFF, 2 * DH_DIFF))
            new["ks"].append(sk.reshape(bsz, seq, H_SB, DH_SB))
            new["vs"].append(sv.reshape(bsz, seq, H_SB, DH_SB))
            x = _outproj(tr, [od, osb], prm["w_out_odd_b"], io, x, li, 2, prm["g_post_mix"])
        act, cs = _ffn_up(tr_ffn, x, li, prm["g_pre_ffn"], prm["w_up_b"], prm["conv_w"], prm["conv_b"], bsz)
        new["conv"].append(cs)
        x = _outproj(tr, [act], prm["w_down_b"], li, x, li, 5, prm["g_post_ffn"])
    st = lambda t: jnp.stack(t, axis=0)
    return x.reshape(bsz, seq, D_MODEL), tuple(st(new[k]) for k in ("ret", "s5r", "s5i", "kd", "vd", "ks", "vs", "conv"))


def _sample_trunk(x_sample, mod_s, prm, past):
    nb = x_sample.shape[0]
    tr = _Trunk(mod_s, 1, nb, 3)
    x = x_sample.reshape(nb, D_MODEL)
    page_table = past["page_table"]
    n_pages = page_table.shape[1]
    q_off = n_pages * PAGE_SIZE
    cos2, sin2 = _rope_tables(jnp.full((1,), q_off))
    bias_d = _bias_by_distance(prm["rel_bias"], q_off + 1)
    key_pos = jnp.arange(q_off).reshape(n_pages, PAGE_SIZE)
    bias_pages = bias_d[q_off - key_pos].reshape(n_pages, 1, PAGE_SIZE * H_DIFF)
    bias0 = jnp.repeat(bias_d[0], 2).reshape(2 * H_DIFF, 1)
    s5_re = past["s5_re"].reshape(past["s5_re"].shape[0], nb, S5_N)
    s5_im = past["s5_im"].reshape(past["s5_im"].shape[0], nb, S5_N)
    conv2d = past["conv"].reshape(DEPTH, nb, (CONV_W - 1) * 2 * D_FF)
    caches = tuple(past[k].reshape(past[k].shape[0], past[k].shape[1], PAGE_SIZE * H_DIFF, 2 * DH_DIFF)
                   for k in ("diff_k", "diff_v"))
    caches += tuple(jnp.transpose(past[k], (0, 1, 3, 4, 2)) for k in ("sb_k", "sb_v"))
    new = {k: [] for k in ("ret", "s5r", "s5i", "kd", "vd", "ks", "vs", "conv")}
    for li in range(DEPTH):
        if li % 2 == 0:
            ie = li // 2
            s5p = prm["s5"][ie]
            q, k, v, g, u = _proj_even(tr, x, li, prm["g_pre_mix"], prm["w_in_even"], cos2, sin2, (f32,) * 5)
            o, y, ret_s, xr, xi = _even_dec(q, k, v, g, u, past["ret"], s5_re, s5_im, ie, prm["ret_gn_g"][ie],
                                            s5p["bw"], s5p["a_re"], s5p["a_im"], s5p["cw"], s5p["d"], s5p["glu"])
            new["ret"].append(ret_s)
            new["s5r"].append(xr.reshape(nb, S5_GROUPS, S5_STATE))
            new["s5i"].append(xi.reshape(nb, S5_GROUPS, S5_STATE))
            x = _outproj(tr, [o, y], prm["w_out_even"], ie, x, li, 2, prm["g_post_mix"])
        else:
            io = li // 2
            dq, dk, dv, sq, sk, sv = _proj_odd(tr, x, li, prm["g_pre_mix"], prm["w_in_odd"], False)
            lam = _diff_lambda(prm["diff_lam"][io], li)
            mix = _attn_dec(page_table, lam, dq, sq, dk, dv, bias_pages, bias0, prm["diff_subln_g"][io].reshape(1, -1),
                            caches, io, _lam_init(li))
            new["kd"].append(dk.reshape(nb, 1, H_DIFF, 2 * DH_DIFF))
            new["vd"].append(dv.reshape(nb, 1, H_DIFF, 2 * DH_DIFF))
            new["ks"].append(sk.reshape(nb, 1, H_SB, DH_SB))
            new["vs"].append(sv.reshape(nb, 1, H_SB, DH_SB))
            x = _outproj(tr, [mix[:, :MIX_W], mix[:, MIX_W:]], prm["w_out_odd"], io, x, li, 2, prm["g_post_mix"])
        act, upv, upg = _ffn_up_dec(x, mod_s, li, prm["g_pre_ffn"], prm["w_up"], prm["conv_w"], prm["conv_b"], conv2d)
        up = jnp.concatenate([upv, upg], axis=-1)
        new["conv"].append(jnp.stack([past["conv"][li][:, CONV_W - 2], up], axis=1))
        x = _outproj(tr, [act], prm["w_down"], li, x, li, 5, prm["g_post_ffn"])
    st = lambda t: jnp.stack(t, axis=0)
    return x.reshape(nb, 1, D_MODEL), tuple(st(new[k]) for k in ("ret", "s5r", "s5i", "kd", "vd", "ks", "vs", "conv"))


def _prepare_params(g_pre_mix, g_post_mix, g_pre_ffn, g_post_ffn, w_in_even, w_out_even, ret_gn_g, s5_lam_re, s5_lam_im,
                    s5_log_step, s5_b_re, s5_b_im, s5_c_re, s5_c_im, s5_d, s5_w_glu, w_in_odd, w_out_odd, diff_lam,
                    diff_subln_g, rel_bias, w_up, conv_w, conv_b, w_down):
    g3 = lambda g: g.reshape(g.shape[0], 1, g.shape[1])
    prm = dict(
        g_pre_mix=g3(g_pre_mix), g_post_mix=g3(g_post_mix), g_pre_ffn=g3(g_pre_ffn), g_post_ffn=g3(g_post_ffn),
        w_in_even=w_in_even, w_out_even=w_out_even, w_in_odd=w_in_odd, w_out_odd=w_out_odd, w_up=w_up, w_down=w_down,
        w_in_even_b=w_in_even.astype(bf16), w_out_even_b=w_out_even.astype(bf16), w_in_odd_b=w_in_odd.astype(bf16),
        w_out_odd_b=w_out_odd.astype(bf16), w_up_b=w_up.astype(bf16), w_down_b=w_down.astype(bf16),
        ret_gn_g=ret_gn_g, diff_lam=diff_lam, diff_subln_g=diff_subln_g, rel_bias=rel_bias,
        conv_w=conv_w, conv_b=g3(conv_b),
    )
    s5 = []
    for ie in range(s5_lam_re.shape[0]):
        a_re, a_im, bw, cw = _s5_params(s5_lam_re[ie], s5_lam_im[ie], s5_log_step[ie], s5_b_re[ie], s5_b_im[ie],
                                        s5_c_re[ie], s5_c_im[ie])
        s5.append(dict(a_re=a_re, a_im=a_im, bw=bw, cw=cw,
                       d=s5_d[ie].reshape(1, S5_WIDTH), glu=s5_w_glu[ie], glu_b=s5_w_glu[ie].astype(bf16)))
    prm["s5"] = s5
    return prm


def kernel(x_prompt, x_sample, c_prompt, c_sample, state_ret, state_s5_re, state_s5_im, cache_diff_k, cache_diff_v, cache_sb_k, cache_sb_v, state_conv, page_table, w_ada, b_ada, g_pre_mix, g_post_mix, g_pre_ffn, g_post_ffn, w_in_even, w_out_even, ret_gn_g, s5_lam_re, s5_lam_im, s5_log_step, s5_b_re, s5_b_im, s5_c_re, s5_c_im, s5_d, s5_w_glu, w_in_odd, w_out_odd, diff_lam, diff_subln_g, rel_bias, w_up, conv_w, conv_b, w_down):
    nb = x_sample.shape[0]
    bsz = x_prompt.shape[0]
    prm = _prepare_params(g_pre_mix, g_post_mix, g_pre_ffn, g_post_ffn, w_in_even, w_out_even, ret_gn_g, s5_lam_re,
                          s5_lam_im, s5_log_step, s5_b_re, s5_b_im, s5_c_re, s5_c_im, s5_d, s5_w_glu, w_in_odd, w_out_odd,
                          diff_lam, diff_subln_g, rel_bias, w_up, conv_w, conv_b, w_down)
    mod = _ada(jnp.concatenate([c_sample, c_prompt], axis=0), w_ada, b_ada)
    mod_s = mod[:, :nb]
    mod_p = mod[:, nb:].reshape(DEPTH, bsz, 1, 6 * D_MODEL)
    past = dict(ret=state_ret, s5_re=state_s5_re, s5_im=state_s5_im, diff_k=cache_diff_k, diff_v=cache_diff_v,
                sb_k=cache_sb_k, sb_v=cache_sb_v, conv=state_conv, page_table=page_table)
    y_prompt, sp = _prompt_trunk(x_prompt, mod_p, prm)
    y_sample, ss = _sample_trunk(x_sample, mod_s, prm, past)
    ret_p, s5r_p, s5i_p, kd_p, vd_p, ks_p, vs_p, conv_p = sp
    ret_s, s5r_s, s5i_s, kd_s, vd_s, ks_s, vs_s, conv_s = ss
    return (y_prompt, y_sample, ret_p, ret_s, s5r_p, s5r_s, s5i_p, s5i_s, kd_p, kd_s, vd_p, vd_s,
            ks_p, ks_s, vs_p, vs_s, conv_p, conv_s)
```
